```python
import jax, jax.numpy as jnp
from jax import lax
import numpy as np

D_MODEL = 1024
BATCH = 16
SEQ = 2048
DEPTH = 4

N_MIXERS = 2
HEAD_DIM = 64
ROPE_THETA = 10000.0
NORM_EPS = 1e-6

NSA_HEADS = 16
NSA_KV_HEADS = 4
CMP_LEN = 32
CMP_STRIDE = 16
CMP_HIDDEN = 256
SLC_LEN = 64
SLC_TOPK = 16
WIN_LEN = 512
FORCE_BONUS = 1e4
SLC_Q_CHUNK = 16
NSA_IN_DIM = NSA_HEADS * HEAD_DIM + 6 * NSA_KV_HEADS * HEAD_DIM + 3 * NSA_HEADS

SWA_HEADS = 16
SWA_KV_HEADS = 2
SWA_WINDOW = 128
SWA_IN_DIM = SWA_HEADS * HEAD_DIM + 2 * SWA_KV_HEADS * HEAD_DIM

Q_BLOCK = 128

D_FF = 2816
CONV_WIDTH = 3

N_A = (DEPTH + 1) // 2
N_B = DEPTH // 2

kernel_name = 'hybrid_nsa_swa_sink_convffn_trunk'


def rms_norm(x, gain):
    xf = x.astype(jnp.float32)
    y = xf * lax.rsqrt(jnp.mean(xf * xf, axis=-1, keepdims=True) + NORM_EPS)
    return (y * gain.astype(jnp.float32)).astype(x.dtype)


def rope_tables(seq):
    pos = jnp.arange(seq, dtype=jnp.float32)
    inv = ROPE_THETA ** (-jnp.arange(0, HEAD_DIM, 2, dtype=jnp.float32) / HEAD_DIM)
    ang = pos[:, None] * inv[None, :]
    return jnp.cos(ang), jnp.sin(ang)


def apply_rope(x, cos, sin):
    c = cos[None, :, None, :].astype(x.dtype)
    s = sin[None, :, None, :].astype(x.dtype)
    x1, x2 = jnp.split(x, 2, axis=-1)
    return jnp.concatenate([x1 * c - x2 * s, x2 * c + x1 * s], axis=-1)


def to_q_groups(q, n_groups):
    b, s, h, d = q.shape
    return q.reshape(b, s, n_groups, h // n_groups, d).transpose(0, 2, 3, 1, 4)


def from_q_groups(o):
    b, g, r, s, d = o.shape
    return o.transpose(0, 3, 1, 2, 4).reshape(b, s, g * r * d)


def masked_softmax(s, mask):
    s = jnp.where(mask, s, -jnp.inf)
    m = jnp.max(s, axis=-1, keepdims=True)
    m = jnp.where(jnp.isfinite(m), m, 0.0)
    p = jnp.exp(s - m)
    d = jnp.sum(p, axis=-1, keepdims=True)
    return p / jnp.where(d > 0.0, d, 1.0)


def banded_attention(q, k, v, window, sinks=None):
    b, g, r, seq, hd = q.shape
    nqb = seq // Q_BLOCK
    span = Q_BLOCK + window
    pad = ((0, 0), (0, 0), (window, 0), (0, 0))
    k_pad = jnp.pad(k, pad)
    v_pad = jnp.pad(v, pad)
    qb = q.reshape(b, g, r, nqb, Q_BLOCK, hd).transpose(3, 0, 1, 2, 4, 5)
    scale = hd ** -0.5

    def block(args):
        blk, q_blk = args
        start = blk * Q_BLOCK
        kb = lax.dynamic_slice_in_dim(k_pad, start, span, axis=2)
        vb = lax.dynamic_slice_in_dim(v_pad, start, span, axis=2)
        qpos = start + jnp.arange(Q_BLOCK)
        kpos = start - window + jnp.arange(span)
        rel = qpos[:, None] - kpos[None, :]
        mask = (rel >= 0) & (rel < window) & (kpos[None, :] >= 0)
        s = jnp.einsum('bgrqd,bgkd->bgrqk', q_blk, kb).astype(jnp.float32) * scale
        if sinks is None:
            p = masked_softmax(s, mask)
        else:
            s = jnp.where(mask, s, -jnp.inf)
            sink = sinks.astype(jnp.float32)[None, :, :, None, None]
            m = jnp.maximum(jnp.max(s, axis=-1, keepdims=True), sink)
            p = jnp.exp(s - m)
            p = p / (jnp.sum(p, axis=-1, keepdims=True) + jnp.exp(sink - m))
        return jnp.einsum('bgrqk,bgkd->bgrqd', p.astype(vb.dtype), vb)

    o = lax.map(block, (jnp.arange(nqb), qb))
    return o.transpose(1, 2, 3, 0, 4, 5).reshape(b, g, r, seq, hd)


def nsa_mixer(h, cos, sin, w_in, cmp_pe, cmp_w1, cmp_w2, gate_b, w_o):
    b, seq, _ = h.shape
    g, r, hd = NSA_KV_HEADS, NSA_HEADS // NSA_KV_HEADS, HEAD_DIM
    sizes = [NSA_HEADS * hd] + [g * hd] * 6 + [3 * NSA_HEADS]
    splits = np.cumsum(sizes)[:-1].tolist()
    q, kc, vc, ks, vs, kw, vw, gt = jnp.split(h @ w_in, splits, axis=-1)

    q = to_q_groups(apply_rope(q.reshape(b, seq, NSA_HEADS, hd), cos, sin), g)

    def kv(t, rotate):
        t = t.reshape(b, seq, g, hd)
        if rotate:
            t = apply_rope(t, cos, sin)
        return t.transpose(0, 2, 1, 3)

    kc, ks, kw = kv(kc, True), kv(ks, True), kv(kw, True)
    vc, vs, vw = kv(vc, False), kv(vs, False), kv(vw, False)
    pos = jnp.arange(seq)
    scale = hd ** -0.5

    n_cmp = (seq - CMP_LEN) // CMP_STRIDE + 1
    cstart = jnp.arange(n_cmp) * CMP_STRIDE
    cidx = cstart[:, None] + jnp.arange(CMP_LEN)[None, :]

    def compress(t, pe, w1, w2):
        blk = (t[:, :, cidx] + pe).reshape(b, g, n_cmp, CMP_LEN * hd)
        return jax.nn.gelu(blk @ w1) @ w2

    kc = compress(kc, cmp_pe[0], cmp_w1[0], cmp_w2[0])
    vc = compress(vc, cmp_pe[1], cmp_w1[1], cmp_w2[1])
    s_cmp = jnp.einsum('bgrqd,bgcd->bgrqc', q, kc).astype(jnp.float32) * scale
    cmask = (cstart + CMP_LEN - 1)[None, :] <= pos[:, None]
    p_cmp = masked_softmax(s_cmp, cmask)
    o_cmp = jnp.einsum('bgrqc,bgcd->bgrqd', p_cmp.astype(vc.dtype), vc)

    n_slc = seq // SLC_LEN
    top_n = min(SLC_TOPK, n_slc)
    sstart = jnp.arange(n_slc) * SLC_LEN
    overlap = ((cstart[:, None] <= sstart[None, :] + SLC_LEN - 1)
               & (cstart[:, None] + CMP_LEN - 1 >= sstart[None, :])).astype(jnp.float32)
    imp = jnp.einsum('bgrqc,cn->bgqn', p_cmp, overlap)
    cur = pos // SLC_LEN
    blk = jnp.arange(n_slc)
    forced = (blk[None, :] == 0) | (blk[None, :] == cur[:, None]) | (blk[None, :] == cur[:, None] - 1)
    causal = blk[None, :] <= cur[:, None]
    score = jnp.where(causal, imp + jnp.where(forced, FORCE_BONUS, 0.0), -1.0)
    top_score, sel = lax.top_k(score, top_n)
    sel_valid = top_score >= 0.0

    nqc = seq // SLC_Q_CHUNK
    q_ch = q.reshape(b, g, r, nqc, SLC_Q_CHUNK, hd).transpose(3, 0, 1, 2, 4, 5)
    sel_ch = sel.reshape(b, g, nqc, SLC_Q_CHUNK, top_n).transpose(2, 0, 1, 3, 4)
    val_ch = sel_valid.reshape(b, g, nqc, SLC_Q_CHUNK, top_n).transpose(2, 0, 1, 3, 4)
    pos_ch = pos.reshape(nqc, SLC_Q_CHUNK)
    gather_rows = jax.vmap(jax.vmap(lambda t, i: t[i]))
    n_keys = top_n * SLC_LEN

    def slc_block(args):
        q_c, sel_c, val_c, qpos = args
        tok = sel_c[..., None] * SLC_LEN + jnp.arange(SLC_LEN)
        tmask = val_c[..., None] & (tok <= qpos[None, None, :, None, None])
        tok = tok.reshape(b, g, SLC_Q_CHUNK, n_keys)
        tmask = tmask.reshape(b, g, SLC_Q_CHUNK, n_keys)
        kg = gather_rows(ks, tok)
        vg = gather_rows(vs, tok)
        s = jnp.einsum('bgrqd,bgqkd->bgrqk', q_c, kg).astype(jnp.float32) * scale
        p = masked_softmax(s, tmask[:, :, None])
        return jnp.einsum('bgrqk,bgqkd->bgrqd', p.astype(vg.dtype), vg)

    o_slc = lax.map(slc_block, (q_ch, sel_ch, val_ch, pos_ch))
    o_slc = o_slc.transpose(1, 2, 3, 0, 4, 5).reshape(b, g, r, seq, hd)

    o_win = banded_attention(q, kw, vw, WIN_LEN)

    gates = jax.nn.sigmoid((gt + gate_b).astype(jnp.float32))
    gates = gates.reshape(b, seq, g, r, 3).transpose(0, 2, 3, 1, 4).astype(q.dtype)
    o = gates[..., 0, None] * o_cmp + gates[..., 1, None] * o_slc + gates[..., 2, None] * o_win
    return from_q_groups(o) @ w_o


def swa_mixer(h, cos, sin, w_qkv, b_qkv, sinks, w_o, b_o):
    b, seq, _ = h.shape
    g, r, hd = SWA_KV_HEADS, SWA_HEADS // SWA_KV_HEADS, HEAD_DIM
    q, k, v = jnp.split(h @ w_qkv + b_qkv, [SWA_HEADS * hd, SWA_HEADS * hd + g * hd], axis=-1)
    q = to_q_groups(apply_rope(q.reshape(b, seq, SWA_HEADS, hd), cos, sin), g)
    k = apply_rope(k.reshape(b, seq, g, hd), cos, sin).transpose(0, 2, 1, 3)
    v = v.reshape(b, seq, g, hd).transpose(0, 2, 1, 3)
    o = banded_attention(q, k, v, SWA_WINDOW, sinks.reshape(g, r))
    return from_q_groups(o) @ w_o + b_o


def causal_depthwise_conv(a, w, bias):
    f = a.shape[-1]
    y = lax.conv_general_dilated(a, w.astype(a.dtype)[:, None, :], window_strides=(1,),
                                 padding=[(CONV_WIDTH - 1, 0)],
                                 dimension_numbers=('NWC', 'WIO', 'NWC'),
                                 feature_group_count=f)
    return y + bias.astype(a.dtype)


def conv_ffn(h, w_gu, conv_w, conv_b, w_down):
    a, u = jnp.split(h @ w_gu, 2, axis=-1)
    a = causal_depthwise_conv(a, conv_w, conv_b)
    return (jax.nn.silu(a) * u) @ w_down


def setup_inputs(seed: int = 0) -> dict:
    key = jax.random.key(seed)
    ks = jax.random.split(key, 20)
    f32 = jnp.float32

    def nrm(k, shape, scale):
        return jax.random.normal(k, shape, f32) * scale

    att_w = NSA_HEADS * HEAD_DIM
    swa_w = SWA_HEADS * HEAD_DIM
    return {
        'x': nrm(ks[0], (BATCH, SEQ, D_MODEL), 1.0),
        'nsa_w_in': nrm(ks[1], (N_A, D_MODEL, NSA_IN_DIM), D_MODEL ** -0.5),
        'nsa_cmp_pe': nrm(ks[2], (N_A, 2, CMP_LEN, HEAD_DIM), 0.02),
        'nsa_cmp_w1': nrm(ks[3], (N_A, 2, CMP_LEN * HEAD_DIM, CMP_HIDDEN), (CMP_LEN * HEAD_DIM) ** -0.5),
        'nsa_cmp_w2': nrm(ks[4], (N_A, 2, CMP_HIDDEN, HEAD_DIM), CMP_HIDDEN ** -0.5),
        'nsa_gate_b': nrm(ks[5], (N_A, 3 * NSA_HEADS), 0.1),
        'nsa_w_o': nrm(ks[6], (N_A, att_w, D_MODEL), att_w ** -0.5),
        'swa_w_qkv': nrm(ks[7], (N_B, D_MODEL, SWA_IN_DIM), D_MODEL ** -0.5),
        'swa_b_qkv': nrm(ks[8], (N_B, SWA_IN_DIM), 0.02),
        'swa_sinks': nrm(ks[9], (N_B, SWA_HEADS), 1.0),
        'swa_w_o': nrm(ks[10], (N_B, swa_w, D_MODEL), swa_w ** -0.5),
        'swa_b_o': nrm(ks[11], (N_B, D_MODEL), 0.02),
        'ffn_w_gu': nrm(ks[12], (DEPTH, D_MODEL, 2 * D_FF), D_MODEL ** -0.5),
        'ffn_conv_w': nrm(ks[13], (DEPTH, CONV_WIDTH, D_FF), CONV_WIDTH ** -0.5),
        'ffn_conv_b': nrm(ks[14], (DEPTH, D_FF), 0.02),
        'ffn_w_down': nrm(ks[15], (DEPTH, D_FF, D_MODEL), D_FF ** -0.5),
        'norm_mix': 1.0 + nrm(ks[16], (DEPTH, D_MODEL), 0.02),
        'norm_ffn': 1.0 + nrm(ks[17], (DEPTH, D_MODEL), 0.02),
        'norm_final': 1.0 + nrm(ks[18], (D_MODEL,), 0.02),
    }


def reference(x, nsa_w_in, nsa_cmp_pe, nsa_cmp_w1, nsa_cmp_w2, nsa_gate_b, nsa_w_o,
              swa_w_qkv, swa_b_qkv, swa_sinks, swa_w_o, swa_b_o,
              ffn_w_gu, ffn_conv_w, ffn_conv_b, ffn_w_down,
              norm_mix, norm_ffn, norm_final):
    cos, sin = rope_tables(x.shape[1])
    for i in range(DEPTH):
        j = i // N_MIXERS
        h = rms_norm(x, norm_mix[i])
        if i % N_MIXERS == 0:
            x = x + nsa_mixer(h, cos, sin, nsa_w_in[j], nsa_cmp_pe[j], nsa_cmp_w1[j],
                              nsa_cmp_w2[j], nsa_gate_b[j], nsa_w_o[j])
        else:
            x = x + swa_mixer(h, cos, sin, swa_w_qkv[j], swa_b_qkv[j], swa_sinks[j],
                              swa_w_o[j], swa_b_o[j])
        h = rms_norm(x, norm_ffn[i])
        x = x + conv_ffn(h, ffn_w_gu[i], ffn_conv_w[i], ffn_conv_b[i], ffn_w_down[i])
    return rms_norm(x, norm_final)
```

```python
import functools

import numpy as np
import jax
import jax.numpy as jnp
from jax import lax
from jax.experimental import pallas as pl
from jax.experimental.pallas import tpu as pltpu

F32 = jnp.float32
BF16 = jnp.bfloat16

D_MODEL = 1024
HEAD_DIM = 64
HALF = HEAD_DIM // 2
ROPE_THETA = 10000.0
NORM_EPS = 1e-6

NSA_HEADS = 16
NSA_KV_HEADS = 4
CMP_LEN = 32
CMP_STRIDE = 16
CMP_HIDDEN = 256
SLC_LEN = 64
SLC_TOPK = 16
WIN_LEN = 512
FORCE_BONUS = 1e4

SWA_HEADS = 16
SWA_KV_HEADS = 2
SWA_WINDOW = 128

D_FF = 2816
CONV_WIDTH = 3

LANES = 128
MXU_N = 256
NEG = -1e30
VMEM_LIMIT = 56 * 1024 * 1024

FF_CHUNK = MXU_N
N_FF_CHUNKS = D_FF // FF_CHUNK
HALO = 16


def _cparams(n_axes):
    return pltpu.CompilerParams(dimension_semantics=("arbitrary",) * n_axes,
                                vmem_limit_bytes=VMEM_LIMIT)


def _rms_bf16(x, gain):
    ms = jnp.mean(x * x, axis=-1, keepdims=True)
    return (x * lax.rsqrt(ms + NORM_EPS) * gain).astype(BF16)


def _dot(a, b):
    return jnp.dot(a, b, preferred_element_type=F32)


def _dot_nt(a, b):
    return lax.dot_general(a, b, (((1,), (1,)), ((), ())), preferred_element_type=F32)


def _rope(a, cos, sin_signed, first_half):
    n = a.shape[1]
    rot = jnp.where(first_half, pltpu.roll(a, n - HALF, 1), pltpu.roll(a, HALF, 1))
    return a * cos + rot * sin_signed


def _nsa_proj_kernel(x_ref, gain_ref, w_ref, cos_ref, sin_ref, gb_ref,
                     q_ref, kc_ref, vc_ref, ks_ref, vs_ref, kw_ref, vw_ref, gate_ref,
                     hn_ref, *, tm, tiles_per_seq):
    hn_ref[...] = _rms_bf16(x_ref[...], gain_ref[...])
    cos = cos_ref[...]
    sin = sin_ref[...]
    lane = lax.broadcasted_iota(jnp.int32, (tm, MXU_N), 1)
    first_half = (lane % HEAD_DIM) < HALF

    def chunk(c):
        return _dot(hn_ref[...], w_ref[:, c * MXU_N:(c + 1) * MXU_N])

    for c in range(4):
        a = _rope(chunk(c), cos, sin, first_half) * (HEAD_DIM ** -0.5)
        q_ref[:, c * MXU_N:(c + 1) * MXU_N] = a.astype(BF16)

    def heads(a, ref):
        for g in range(NSA_KV_HEADS):
            ref[0, g] = a[:, g * HEAD_DIM:(g + 1) * HEAD_DIM].astype(BF16)

    heads(_rope(chunk(4), cos, sin, first_half), kc_ref)
    heads(chunk(5), vc_ref)
    a = _rope(chunk(6), cos, sin, first_half)
    t = pl.program_id(0) % tiles_per_seq
    row = lax.broadcasted_iota(jnp.int32, (tm, HEAD_DIM), 0)
    col = lax.broadcasted_iota(jnp.int32, (tm, HEAD_DIM), 1)
    onehot = jnp.where(col == (t * tm + row) // SLC_LEN, 1.0, 0.0).astype(BF16)
    for g in range(NSA_KV_HEADS):
        kg = a[:, g * HEAD_DIM:(g + 1) * HEAD_DIM].astype(BF16)
        ks_ref[0, g] = jnp.concatenate([kg, onehot], axis=1)
    heads(chunk(7), vs_ref)
    heads(_rope(chunk(8), cos, sin, first_half), kw_ref)
    heads(chunk(9), vw_ref)
    z = _dot(hn_ref[...], w_ref[:, 10 * MXU_N:10 * MXU_N + LANES]) + gb_ref[...]
    gate_ref[...] = 1.0 / (1.0 + jnp.exp(-z))


def _nsa_proj(x, gain, w, cos, sin, gate_b, *, batch, seq, tm=512):
    tokens = batch * seq
    tps = seq // tm
    g = NSA_KV_HEADS
    kv64 = jax.ShapeDtypeStruct((batch, g, seq, HEAD_DIM), BF16)
    kv128 = jax.ShapeDtypeStruct((batch, g, seq, LANES), BF16)
    kv_spec64 = pl.BlockSpec((1, g, tm, HEAD_DIM), lambda i: (i // tps, 0, i % tps, 0))
    kv_spec128 = pl.BlockSpec((1, g, tm, LANES), lambda i: (i // tps, 0, i % tps, 0))
    row_spec = lambda n: pl.BlockSpec((tm, n), lambda i: (i, 0))
    tab_spec = pl.BlockSpec((tm, MXU_N), lambda i: (i % tps, 0))
    full = lambda a: pl.BlockSpec(a.shape, lambda i: (0,) * a.ndim)
    return pl.pallas_call(
        functools.partial(_nsa_proj_kernel, tm=tm, tiles_per_seq=tps),
        grid=(tokens // tm,),
        in_specs=[row_spec(D_MODEL), full(gain), full(w), tab_spec, tab_spec, full(gate_b)],
        out_specs=[row_spec(D_MODEL), kv_spec64, kv_spec64, kv_spec128, kv_spec64, kv_spec64,
                   kv_spec64, row_spec(LANES)],
        out_shape=[jax.ShapeDtypeStruct((tokens, D_MODEL), BF16), kv64, kv64, kv128, kv64, kv64,
                   kv64, jax.ShapeDtypeStruct((tokens, LANES), F32)],
        scratch_shapes=[pltpu.VMEM((tm, D_MODEL), BF16)],
        compiler_params=_cparams(1),
        name="nsa_proj",
    )(x, gain, w, cos, sin, gate_b)


def _swa_proj_kernel(x_ref, gain_ref, w_ref, b_ref, cos_ref, sin_ref,
                     q_ref, k_ref, v_ref, hn_ref, *, tm):
    hn_ref[...] = _rms_bf16(x_ref[...], gain_ref[...])
    cos = cos_ref[...]
    sin = sin_ref[...]
    lane = lax.broadcasted_iota(jnp.int32, (tm, MXU_N), 1)
    first_half = (lane % HEAD_DIM) < HALF
    for c in range(4):
        sl = slice(c * MXU_N, (c + 1) * MXU_N)
        a = _dot(hn_ref[...], w_ref[:, sl]) + b_ref[:, sl]
        q_ref[:, sl] = (_rope(a, cos, sin, first_half) * (HEAD_DIM ** -0.5)).astype(BF16)
    sl = slice(4 * MXU_N, 5 * MXU_N)
    a = _dot(hn_ref[...], w_ref[:, sl]) + b_ref[:, sl]
    ak = _rope(a, cos, sin, first_half)
    for g in range(SWA_KV_HEADS):
        k_ref[0, g] = ak[:, g * HEAD_DIM:(g + 1) * HEAD_DIM].astype(BF16)
        v_ref[0, g] = a[:, (SWA_KV_HEADS + g) * HEAD_DIM:(SWA_KV_HEADS + g + 1) * HEAD_DIM].astype(BF16)


def _swa_proj(x, gain, w, b, cos, sin, *, batch, seq, tm=512):
    tokens = batch * seq
    tps = seq // tm
    g = SWA_KV_HEADS
    kv = jax.ShapeDtypeStruct((batch, g, seq, HEAD_DIM), BF16)
    kv_spec = pl.BlockSpec((1, g, tm, HEAD_DIM), lambda i: (i // tps, 0, i % tps, 0))
    row_spec = lambda n: pl.BlockSpec((tm, n), lambda i: (i, 0))
    tab_spec = pl.BlockSpec((tm, MXU_N), lambda i: (i % tps, 0))
    full = lambda a: pl.BlockSpec(a.shape, lambda i: (0,) * a.ndim)
    return pl.pallas_call(
        functools.partial(_swa_proj_kernel, tm=tm),
        grid=(tokens // tm,),
        in_specs=[row_spec(D_MODEL), full(gain), full(w), full(b), tab_spec, tab_spec],
        out_specs=[row_spec(D_MODEL), kv_spec, kv_spec],
        out_shape=[jax.ShapeDtypeStruct((tokens, D_MODEL), BF16), kv, kv],
        scratch_shapes=[pltpu.VMEM((tm, D_MODEL), BF16)],
        compiler_params=_cparams(1),
        name="swa_proj",
    )(x, gain, w, b, cos, sin)


def _gelu_tanh(x):
    return 0.5 * x * (1.0 + jnp.tanh(np.sqrt(2.0 / np.pi).astype(np.float32) * (x + 0.044715 * (x * x * x))))


def _compress_kernel(xk_ref, xv_ref, pe_ref, w1_ref, w2_ref, ok_ref, ov_ref, *, rows):
    half = CMP_STRIDE * HEAD_DIM
    for t, (x_ref, o_ref) in enumerate(((xk_ref, ok_ref), (xv_ref, ov_ref))):
        x = x_ref[...].astype(F32)
        lo = _dot((x + pe_ref[t, 0:1]).astype(BF16), w1_ref[t, :half])
        hi = _dot((x + pe_ref[t, 1:2]).astype(BF16), w1_ref[t, half:])
        pre = lo + pltpu.roll(hi, rows - 1, 0)
        o_ref[...] = _dot(_gelu_tanh(pre).astype(BF16), w2_ref[t]).astype(BF16)


def _compress(xk, xv, pe, w1, w2, *, rows=1024):
    n = xk.shape[0]
    rows = min(rows, n)
    row_spec = lambda w: pl.BlockSpec((rows, w), lambda i: (i, 0))
    full = lambda a: pl.BlockSpec(a.shape, lambda i: (0,) * a.ndim)
    out = jax.ShapeDtypeStruct((n, HEAD_DIM), BF16)
    return pl.pallas_call(
        functools.partial(_compress_kernel, rows=rows),
        grid=(n // rows,),
        in_specs=[row_spec(xk.shape[1]), row_spec(xk.shape[1]), full(pe), full(w1), full(w2)],
        out_specs=[row_spec(HEAD_DIM), row_spec(HEAD_DIM)],
        out_shape=[out, out],
        compiler_params=_cparams(1),
        name="nsa_compress",
    )(xk, xv, pe, w1, w2)


def _gate_col(gates, lane_idx, target):
    return jnp.sum(jnp.where(lane_idx == target, gates, 0.0), axis=1, keepdims=True)


def _cmp_attn_kernel(q_ref, kc_ref, vc_ref, gate_ref, ovl_ref, o_ref, selb_ref, *, tq, n_slc):
    g = pl.program_id(1)
    qi = pl.program_id(2)
    r_heads = NSA_HEADS // NSA_KV_HEADS
    q = q_ref[0]
    kc = kc_ref[0, 0]
    vc = vc_ref[0, 0]
    gates = gate_ref[0]
    lane = lax.broadcasted_iota(jnp.int32, (tq, LANES), 1)
    qpos = qi * tq + lax.broadcasted_iota(jnp.int32, (tq, LANES), 0)
    allowed = (lane * CMP_STRIDE + (CMP_LEN - 1)) <= qpos
    psum = jnp.zeros((tq, LANES), F32)
    outs = []
    for r in range(r_heads):
        s = _dot_nt(q[:, r * HEAD_DIM:(r + 1) * HEAD_DIM], kc)
        s = jnp.where(allowed, s, NEG)
        m = jnp.max(s, axis=1, keepdims=True)
        p = jnp.where(allowed, jnp.exp(s - m), 0.0)
        d = jnp.sum(p, axis=1, keepdims=True)
        p = p / jnp.where(d > 0.0, d, 1.0)
        psum = psum + p
        o = _dot(p.astype(BF16), vc)
        outs.append(o * _gate_col(gates, lane, g * r_heads + r))
    o_ref[0] = jnp.concatenate(outs, axis=1).astype(BF16)

    imp_t = _dot_nt(ovl_ref[...], psum.astype(BF16))[:n_slc]
    blk = lax.broadcasted_iota(jnp.int32, (n_slc, tq), 0)
    cur = (qi * tq + lax.broadcasted_iota(jnp.int32, (n_slc, tq), 1)) // SLC_LEN
    forced = (blk == 0) | (blk == cur) | (blk == cur - 1)
    causal = blk <= cur
    score = jnp.where(causal, imp_t + jnp.where(forced, FORCE_BONUS, 0.0), -1.0)
    rank = jnp.zeros((n_slc, tq), F32)
    for m_blk in range(n_slc):
        sm = score[m_blk:m_blk + 1, :]
        rank = rank + jnp.where(blk > m_blk, jnp.where(sm >= score, 1.0, 0.0),
                                jnp.where(sm > score, 1.0, 0.0))
    selected = causal & (rank < float(SLC_TOPK))
    bias_t = jnp.where(selected, 0.0, NEG)
    bias_t = jnp.concatenate([bias_t, jnp.zeros((LANES - n_slc, tq), F32)], axis=0)
    selb_ref[0, 0] = bias_t.T.astype(BF16)


def _cmp_attn(q, kc, vc, gates, ovl_t, *, batch, seq, tq=128):
    g = NSA_KV_HEADS
    gw = (NSA_HEADS // g) * HEAD_DIM
    n_cmp = kc.shape[2]
    return pl.pallas_call(
        functools.partial(_cmp_attn_kernel, tq=tq, n_slc=seq // SLC_LEN),
        grid=(batch, g, seq // tq),
        in_specs=[pl.BlockSpec((1, tq, gw), lambda b, h, i: (b, i, h)),
                  pl.BlockSpec((1, 1, n_cmp, HEAD_DIM), lambda b, h, i: (b, h, 0, 0)),
                  pl.BlockSpec((1, 1, n_cmp, HEAD_DIM), lambda b, h, i: (b, h, 0, 0)),
                  pl.BlockSpec((1, tq, LANES), lambda b, h, i: (b, i, 0)),
                  pl.BlockSpec(ovl_t.shape, lambda b, h, i: (0, 0))],
        out_specs=[pl.BlockSpec((1, tq, gw), lambda b, h, i: (b, i, h)),
                   pl.BlockSpec((1, 1, tq, LANES), lambda b, h, i: (b, h, i, 0))],
        out_shape=[jax.ShapeDtypeStruct((batch, seq, NSA_HEADS * HEAD_DIM), BF16),
                   jax.ShapeDtypeStruct((batch, g, seq, LANES), BF16)],
        compiler_params=_cparams(3),
        name="nsa_cmp_attn",
    )(q, kc, vc, gates, ovl_t)


def _flash_kernel(*refs, tq, tk, r_heads, window, use_selb, use_sink, gate_base, n_heads_total):
    it = iter(refs)
    q_ref, k_ref, v_ref = next(it), next(it), next(it)
    gate_ref = next(it) if gate_base is not None else None
    selb_ref = next(it) if use_selb else None
    sink_ref = next(it) if use_sink else None
    o_ref = next(it)
    qs_ref, m_ref, l_ref, acc_ref = next(it), next(it), next(it), next(it)

    g = pl.program_id(1)
    qi = pl.program_id(2)
    q = q_ref[0]
    for r in range(r_heads):
        qr = q[:, r * HEAD_DIM:(r + 1) * HEAD_DIM]
        if use_selb:
            qr = jnp.concatenate([qr, selb_ref[0, 0][:, :HEAD_DIM]], axis=1)
        qs_ref[r * tq:(r + 1) * tq, :] = qr
    if use_sink:
        m_ref[...] = sink_ref[0]
        l_ref[...] = jnp.ones_like(l_ref)
    else:
        m_ref[...] = jnp.full_like(m_ref, NEG)
        l_ref[...] = jnp.zeros_like(l_ref)
    acc_ref[...] = jnp.zeros_like(acc_ref)

    q0 = qi * tq
    rel = (lax.broadcasted_iota(jnp.int32, (tq, tk), 0)
           - lax.broadcasted_iota(jnp.int32, (tq, tk), 1))
    hi = (q0 + tq - 1) // tk
    lo = 0 if window is None else jnp.maximum(q0 - (window - 1), 0) // tk

    def body(kt, carry):
        k0 = pl.multiple_of(kt * tk, tk)
        k = k_ref[0, 0, pl.ds(k0, tk), :]
        v = v_ref[0, 0, pl.ds(k0, tk), :]
        s = _dot_nt(qs_ref[...], k)
        d = rel + (q0 - k0)
        ok = d >= 0
        if window is not None:
            ok = ok & (d < window)
        bias = jnp.where(ok, 0.0, NEG)
        s = (s.reshape(r_heads, tq, tk) + bias[None]).reshape(r_heads * tq, tk)
        m_old = m_ref[...]
        m_new = jnp.maximum(m_old, jnp.max(s, axis=1, keepdims=True))
        alpha = jnp.exp(m_old - m_new)
        p = jnp.exp(s - m_new)
        l_ref[...] = alpha * l_ref[...] + jnp.sum(p, axis=1, keepdims=True)
        acc_ref[...] = alpha * acc_ref[...] + _dot(p.astype(BF16), v)
        m_ref[...] = m_new
        return carry

    lax.fori_loop(lo, hi + 1, body, 0)

    inv = 1.0 / l_ref[...]
    if gate_base is not None:
        gates = gate_ref[0]
        lane = lax.broadcasted_iota(jnp.int32, (tq, LANES), 1)
    outs = []
    for r in range(r_heads):
        o = acc_ref[r * tq:(r + 1) * tq, :] * inv[r * tq:(r + 1) * tq, :]
        if gate_base is not None:
            o = o * _gate_col(gates, lane, gate_base * n_heads_total + g * r_heads + r)
        outs.append(o)
    o_ref[0] = jnp.concatenate(outs, axis=1).astype(BF16)


def _flash(q, k, v, *, batch, seq, n_heads, n_kv, tq, tk, window=None, gates=None, gate_base=None,
           selb=None, sinks=None, name="flash"):
    r_heads = n_heads // n_kv
    gw = r_heads * HEAD_DIM
    dk = k.shape[-1]
    in_specs = [pl.BlockSpec((1, tq, gw), lambda b, h, i: (b, i, h)),
                pl.BlockSpec((1, 1, seq, dk), lambda b, h, i: (b, h, 0, 0)),
                pl.BlockSpec((1, 1, seq, HEAD_DIM), lambda b, h, i: (b, h, 0, 0))]
    args = [q, k, v]
    if gates is not None:
        in_specs.append(pl.BlockSpec((1, tq, LANES), lambda b, h, i: (b, i, 0)))
        args.append(gates)
    if selb is not None:
        in_specs.append(pl.BlockSpec((1, 1, tq, LANES), lambda b, h, i: (b, h, i, 0)))
        args.append(selb)
    if sinks is not None:
        in_specs.append(pl.BlockSpec((1, r_heads * tq, 1), lambda b, h, i: (h, 0, 0)))
        args.append(sinks)
    rows = r_heads * tq
    return pl.pallas_call(
        functools.partial(_flash_kernel, tq=tq, tk=tk, r_heads=r_heads, window=window,
                          use_selb=selb is not None, use_sink=sinks is not None,
                          gate_base=gate_base if gates is not None else None,
                          n_heads_total=n_heads),
        grid=(batch, n_kv, seq // tq),
        in_specs=in_specs,
        out_specs=pl.BlockSpec((1, tq, gw), lambda b, h, i: (b, i, h)),
        out_shape=jax.ShapeDtypeStruct((batch, seq, n_heads * HEAD_DIM), BF16),
        scratch_shapes=[pltpu.VMEM((rows, dk), BF16), pltpu.VMEM((rows, 1), F32),
                        pltpu.VMEM((rows, 1), F32), pltpu.VMEM((rows, HEAD_DIM), F32)],
        compiler_params=_cparams(3),
        name=name,
    )(*args)


def _out_proj_kernel(*refs, n_in, use_bias):
    o_refs = refs[:n_in]
    x_ref, w_ref = refs[n_in], refs[n_in + 1]
    b_ref = refs[n_in + 2] if use_bias else None
    out_ref = refs[-1]
    o = o_refs[0][...]
    if n_in > 1:
        o = o.astype(F32)
        for r in o_refs[1:]:
            o = o + r[...].astype(F32)
        o = o.astype(BF16)
    y = _dot(o, w_ref[...]) + x_ref[...]
    if use_bias:
        y = y + b_ref[...]
    out_ref[...] = y


def _out_proj(os_, x, w, b=None, *, tm=512):
    tokens = x.shape[0]
    row = pl.BlockSpec((tm, D_MODEL), lambda i: (i, 0))
    full = lambda a: pl.BlockSpec(a.shape, lambda i: (0,) * a.ndim)
    args = list(os_) + [x, w] + ([b] if b is not None else [])
    in_specs = [row] * (len(os_) + 1) + [full(w)] + ([full(b)] if b is not None else [])
    return pl.pallas_call(
        functools.partial(_out_proj_kernel, n_in=len(os_), use_bias=b is not None),
        grid=(tokens // tm,),
        in_specs=in_specs,
        out_specs=row,
        out_shape=jax.ShapeDtypeStruct((tokens, D_MODEL), F32),
        compiler_params=_cparams(1),
        name="out_proj",
    )(*args)


def _ffn_kernel(x_ref, halo_ref, gain_ref, wg_ref, wu_ref, cw_ref, cb_ref, wd_ref, fin_ref,
                o_ref, hn_ref, acc_ref, *, tm, tiles_per_seq, final_norm):
    gain = gain_ref[...]
    hn_ref[0:HALO] = _rms_bf16(halo_ref[...], gain)
    hn_ref[HALO:] = _rms_bf16(x_ref[...], gain)
    acc_ref[...] = jnp.zeros_like(acc_ref)
    seq_start = (pl.program_id(0) % tiles_per_seq) == 0
    row = lax.broadcasted_iota(jnp.int32, (tm + HALO, FF_CHUNK), 0)
    keep = jnp.logical_not(seq_start & (row < HALO))

    def body(c, carry):
        gpre = _dot(hn_ref[...], wg_ref[c])
        u = _dot(hn_ref[HALO:], wu_ref[c])
        gpre = jnp.where(keep, gpre, 0.0)
        cw = cw_ref[c]
        y = (cw[2:3] * gpre[HALO:]
             + cw[1:2] * pltpu.roll(gpre, 1, 0)[HALO:]
             + cw[0:1] * pltpu.roll(gpre, 2, 0)[HALO:]
             + cb_ref[c])
        act = y * (1.0 / (1.0 + jnp.exp(-y))) * u
        acc_ref[...] += _dot(act.astype(BF16), wd_ref[c])
        return carry

    lax.fori_loop(0, N_FF_CHUNKS, body, 0)
    y = x_ref[...] + acc_ref[...]
    if final_norm:
        ms = jnp.mean(y * y, axis=-1, keepdims=True)
        y = y * lax.rsqrt(ms + NORM_EPS) * fin_ref[...]
    o_ref[...] = y


def _ffn(x, gain, wg, wu, cw, cb, wd, fin_gain, *, seq, final_norm, tm=512):
    tokens = x.shape[0]
    tps = seq // tm
    hb = tm // HALO
    row = pl.BlockSpec((tm, D_MODEL), lambda i: (i, 0))
    halo = pl.BlockSpec((HALO, D_MODEL), lambda i: (jnp.maximum(i * hb - 1, 0), 0))
    full = lambda a: pl.BlockSpec(a.shape, lambda i: (0,) * a.ndim)
    return pl.pallas_call(
        functools.partial(_ffn_kernel, tm=tm, tiles_per_seq=tps, final_norm=final_norm),
        grid=(tokens // tm,),
        in_specs=[row, halo, full(gain), full(wg), full(wu), full(cw), full(cb), full(wd),
                  full(fin_gain)],
        out_specs=row,
        out_shape=jax.ShapeDtypeStruct((tokens, D_MODEL), F32),
        scratch_shapes=[pltpu.VMEM((tm + HALO, D_MODEL), BF16), pltpu.VMEM((tm, D_MODEL), F32)],
        compiler_params=_cparams(1),
        name="conv_ffn",
    )(x, x, gain, wg, wu, cw, cb, wd, fin_gain)


def _rope_tables(seq):
    pos = jnp.arange(seq, dtype=F32)
    inv = ROPE_THETA ** (-jnp.arange(0, HEAD_DIM, 2, dtype=F32) / HEAD_DIM)
    ang = pos[:, None] * inv[None, :]
    cos, sin = jnp.cos(ang), jnp.sin(ang)
    reps = MXU_N // HEAD_DIM
    cos_t = jnp.tile(jnp.concatenate([cos, cos], axis=1), (1, reps))
    sin_t = jnp.tile(jnp.concatenate([-sin, sin], axis=1), (1, reps))
    return cos_t, sin_t


def _overlap_t(seq):
    n_cmp = (seq - CMP_LEN) // CMP_STRIDE + 1
    cstart = np.arange(LANES) * CMP_STRIDE
    sstart = np.arange(LANES) * SLC_LEN
    ovl = ((cstart[None, :] <= sstart[:, None] + SLC_LEN - 1)
           & (cstart[None, :] + CMP_LEN - 1 >= sstart[:, None])
           & (np.arange(LANES)[None, :] < n_cmp) & (np.arange(LANES)[:, None] < seq // SLC_LEN))
    return jnp.asarray(ovl, dtype=BF16)


def _nsa_layer(x, gain, cos, sin, w_in, cmp_pe, cmp_w1, cmp_w2, gate_b, w_o, ovl_t, *, batch, seq):
    n_qkv = NSA_HEADS * HEAD_DIM + 6 * NSA_KV_HEADS * HEAD_DIM
    perm = np.arange(3 * NSA_HEADS).reshape(NSA_HEADS, 3).T.reshape(-1)
    pad = LANES - 3 * NSA_HEADS
    w = jnp.concatenate([w_in[:, :n_qkv], w_in[:, n_qkv:][:, perm],
                         jnp.zeros((D_MODEL, pad), F32)], axis=1).astype(BF16)
    gb = jnp.concatenate([gate_b[perm], jnp.zeros((pad,), F32)])[None, :]
    q, kc, vc, ks, vs, kw, vw, gates = _nsa_proj(x, gain[None, :], w, cos, sin, gb, batch=batch, seq=seq)

    rows16 = seq // CMP_STRIDE
    flat = lambda t: t.reshape(batch * NSA_KV_HEADS * rows16, CMP_STRIDE * HEAD_DIM)
    pe = cmp_pe.reshape(2, 2, CMP_STRIDE * HEAD_DIM)
    kcc, vcc = _compress(flat(kc), flat(vc), pe, cmp_w1.astype(BF16), cmp_w2.astype(BF16))
    kcc = kcc.reshape(batch, NSA_KV_HEADS, rows16, HEAD_DIM)
    vcc = vcc.reshape(batch, NSA_KV_HEADS, rows16, HEAD_DIM)

    q3 = q.reshape(batch, seq, NSA_HEADS * HEAD_DIM)
    g3 = gates.reshape(batch, seq, LANES)
    o_cmp, selb = _cmp_attn(q3, kcc, vcc, g3, ovl_t, batch=batch, seq=seq)
    common = dict(batch=batch, seq=seq, n_heads=NSA_HEADS, n_kv=NSA_KV_HEADS, gates=g3)
    o_slc = _flash(q3, ks, vs, tq=256, tk=256, selb=selb, gate_base=1, name="nsa_slc_attn", **common)
    o_win = _flash(q3, kw, vw, tq=256, tk=256, window=WIN_LEN, gate_base=2, name="nsa_win_attn", **common)
    tokens = batch * seq
    os_ = [o.reshape(tokens, NSA_HEADS * HEAD_DIM) for o in (o_cmp, o_slc, o_win)]
    return _out_proj(os_, x, w_o.astype(BF16))


def _swa_layer(x, gain, cos, sin, w_qkv, b_qkv, sinks, w_o, b_o, *, batch, seq):
    tq = 128
    q, k, v = _swa_proj(x, gain[None, :], w_qkv.astype(BF16), b_qkv[None, :], cos, sin,
                        batch=batch, seq=seq)
    r_heads = SWA_HEADS // SWA_KV_HEADS
    sink_rows = jnp.repeat(sinks.reshape(SWA_KV_HEADS, r_heads), tq, axis=1)[:, :, None]
    o = _flash(q.reshape(batch, seq, SWA_HEADS * HEAD_DIM), k, v, batch=batch, seq=seq,
               n_heads=SWA_HEADS, n_kv=SWA_KV_HEADS, tq=tq, tk=128, window=SWA_WINDOW,
               sinks=sink_rows, name="swa_attn")
    return _out_proj([o.reshape(batch * seq, SWA_HEADS * HEAD_DIM)], x, w_o.astype(BF16), b_o[None, :])


def _ffn_layer(x, gain, w_gu, conv_w, conv_b, w_down, fin_gain, *, seq, final_norm):
    chunks = lambda w: w.reshape(D_MODEL, N_FF_CHUNKS, FF_CHUNK).transpose(1, 0, 2).astype(BF16)
    wg = chunks(w_gu[:, :D_FF])
    wu = chunks(w_gu[:, D_FF:])
    wd = w_down.reshape(N_FF_CHUNKS, FF_CHUNK, D_MODEL).astype(BF16)
    cw = conv_w.reshape(CONV_WIDTH, N_FF_CHUNKS, FF_CHUNK).transpose(1, 0, 2)
    cb = conv_b.reshape(N_FF_CHUNKS, 1, FF_CHUNK)
    return _ffn(x, gain[None, :], wg, wu, cw, cb, wd, fin_gain[None, :], seq=seq, final_norm=final_norm)


def kernel(x, nsa_w_in, nsa_cmp_pe, nsa_cmp_w1, nsa_cmp_w2, nsa_gate_b, nsa_w_o,
           swa_w_qkv, swa_b_qkv, swa_sinks, swa_w_o, swa_b_o,
           ffn_w_gu, ffn_conv_w, ffn_conv_b, ffn_w_down,
           norm_mix, norm_ffn, norm_final):
    batch, seq, _ = x.shape
    depth = ffn_w_gu.shape[0]
    cos, sin = _rope_tables(seq)
    ovl_t = _overlap_t(seq)
    h = x.reshape(batch * seq, D_MODEL)
    for i in range(depth):
        j = i // 2
        if i % 2 == 0:
            h = _nsa_layer(h, norm_mix[i], cos, sin, nsa_w_in[j], nsa_cmp_pe[j], nsa_cmp_w1[j],
                           nsa_cmp_w2[j], nsa_gate_b[j], nsa_w_o[j], ovl_t, batch=batch, seq=seq)
        else:
            h = _swa_layer(h, norm_mix[i], cos, sin, swa_w_qkv[j], swa_b_qkv[j], swa_sinks[j],
                           swa_w_o[j], swa_b_o[j], batch=batch, seq=seq)
        h = _ffn_layer(h, norm_ffn[i], ffn_w_gu[i], ffn_conv_w[i], ffn_conv_b[i], ffn_w_down[i],
                       norm_final, seq=seq, final_norm=(i == depth - 1))
    return h.reshape(batch, seq, D_MODEL)
```

```python
import functools

import numpy as np
import jax
import jax.numpy as jnp
from jax import lax
from jax.experimental import pallas as pl
from jax.experimental.pallas import tpu as pltpu

F32 = jnp.float32
BF16 = jnp.bfloat16

D_MODEL = 1024
HEAD_DIM = 64
HALF = HEAD_DIM // 2
ROPE_THETA = 10000.0
NORM_EPS = 1e-6

NSA_HEADS = 16
NSA_KV_HEADS = 4
NSA_R = NSA_HEADS // NSA_KV_HEADS
CMP_LEN = 32
CMP_STRIDE = 16
CMP_HIDDEN = 256
SLC_LEN = 64
SLC_TOPK = 16
WIN_LEN = 512
FORCE_BONUS = 1e4

SWA_HEADS = 16
SWA_KV_HEADS = 2
SWA_R = SWA_HEADS // SWA_KV_HEADS
SWA_WINDOW = 128

D_FF = 2816
CONV_WIDTH = 3

LANES = 128
MXU_N = 256
NEG = -1e30
LOG2E = float(np.log2(np.e))
Q_SCALE = HEAD_DIM ** -0.5 * LOG2E
VMEM_LIMIT = 56 * 1024 * 1024

FF_CHUNK = MXU_N
N_FF_CHUNKS = D_FF // FF_CHUNK
HALO = 16

PROJ_TM = 512
NSA_TQ = 256
WIN_TILES = WIN_LEN // NSA_TQ + 1
SWA_TQ = 128
SWA_TILES = SWA_WINDOW // SWA_TQ + 1


def _cparams(n_axes):
    return pltpu.CompilerParams(dimension_semantics=("arbitrary",) * n_axes,
                                vmem_limit_bytes=VMEM_LIMIT)


def _full(a):
    return pl.BlockSpec(a.shape, lambda *_: (0,) * a.ndim)


def _rms_bf16(x, gain):
    ms = jnp.mean(x * x, axis=-1, keepdims=True)
    return (x * lax.rsqrt(ms + NORM_EPS) * gain).astype(BF16)


def _dot(a, b):
    return jnp.dot(a, b, preferred_element_type=F32)


def _dot_nt(a, b):
    return lax.dot_general(a, b, (((1,), (1,)), ((), ())), preferred_element_type=F32)


def _rope(a, cos, sin_signed, first_half):
    n = a.shape[1]
    rot = jnp.where(first_half, pltpu.roll(a, n - HALF, 1), pltpu.roll(a, HALF, 1))
    return a * cos + rot * sin_signed


def _store_heads(a, ref, base, extra, low):
    for j in range(a.shape[1] // HEAD_DIM):
        slab = a[:, (j // 2) * LANES:(j // 2 + 1) * LANES]
        if j % 2:
            slab = pltpu.roll(slab, HEAD_DIM, 1)
        ref[0, base + j] = jnp.where(low, slab, extra).astype(BF16)


def _nsa_proj_kernel(x_ref, gain_ref, w_ref, cos_ref, sin_ref, gb_ref,
                     q_ref, kc_ref, vc_ref, ks_ref, vs_ref, kw_ref, vw_ref, gate_ref,
                     hn_ref, *, tm, tiles_per_seq):
    hn_ref[...] = _rms_bf16(x_ref[...], gain_ref[...])
    cos = cos_ref[...]
    sin = sin_ref[...]
    lane = lax.broadcasted_iota(jnp.int32, (tm, MXU_N), 1)
    first_half = (lane % HEAD_DIM) < HALF
    lane1 = lax.broadcasted_iota(jnp.int32, (tm, LANES), 1)
    low = lane1 < HEAD_DIM
    zeros = jnp.zeros((tm, LANES), F32)
    ones_lane = jnp.where(lane1 >= HEAD_DIM, 1.0, 0.0)
    t = pl.program_id(0) % tiles_per_seq
    blk = (t * tm + lax.broadcasted_iota(jnp.int32, (tm, LANES), 0)) // SLC_LEN
    onehot = jnp.where(lane1 - HEAD_DIM == blk, 1.0, 0.0)

    def chunk(c):
        return _dot(hn_ref[...], w_ref[:, c * MXU_N:(c + 1) * MXU_N])

    def plain_heads(a, ref):
        for g in range(NSA_KV_HEADS):
            ref[0, g] = a[:, g * HEAD_DIM:(g + 1) * HEAD_DIM].astype(BF16)

    for c in range(4):
        a = _rope(chunk(c), cos, sin, first_half) * Q_SCALE
        _store_heads(a, q_ref, c * NSA_R, zeros, low)
    plain_heads(_rope(chunk(4), cos, sin, first_half), kc_ref)
    plain_heads(chunk(5), vc_ref)
    _store_heads(_rope(chunk(6), cos, sin, first_half), ks_ref, 0, onehot, low)
    _store_heads(chunk(7), vs_ref, 0, ones_lane, low)
    _store_heads(_rope(chunk(8), cos, sin, first_half), kw_ref, 0, zeros, low)
    _store_heads(chunk(9), vw_ref, 0, ones_lane, low)
    z = _dot(hn_ref[...], w_ref[:, 10 * MXU_N:10 * MXU_N + LANES]) + gb_ref[...]
    sig = 1.0 / (1.0 + jnp.exp(-z))
    per_group = 3 * NSA_R
    for g in range(NSA_KV_HEADS):
        gate_ref[0, g] = sig if g == 0 else pltpu.roll(sig, LANES - per_group * g, 1)


def _nsa_proj(x, gain, w, cos, sin, gate_b, *, batch, seq):
    tm = PROJ_TM
    tokens = batch * seq
    tps = seq // tm
    g = NSA_KV_HEADS
    head_spec = lambda n, w_: pl.BlockSpec((1, n, tm, w_), lambda i: (i // tps, 0, i % tps, 0))
    head_shape = lambda n, w_, dt=BF16: jax.ShapeDtypeStruct((batch, n, seq, w_), dt)
    row_spec = pl.BlockSpec((tm, D_MODEL), lambda i: (i, 0))
    tab_spec = pl.BlockSpec((tm, MXU_N), lambda i: (i % tps, 0))
    return pl.pallas_call(
        functools.partial(_nsa_proj_kernel, tm=tm, tiles_per_seq=tps),
        grid=(tokens // tm,),
        in_specs=[row_spec, _full(gain), _full(w), tab_spec, tab_spec, _full(gate_b)],
        out_specs=[head_spec(NSA_HEADS, LANES), head_spec(g, HEAD_DIM), head_spec(g, HEAD_DIM),
                   head_spec(g, LANES), head_spec(g, LANES), head_spec(g, LANES), head_spec(g, LANES),
                   head_spec(g, LANES)],
        out_shape=[head_shape(NSA_HEADS, LANES), head_shape(g, HEAD_DIM), head_shape(g, HEAD_DIM),
                   head_shape(g, LANES), head_shape(g, LANES), head_shape(g, LANES), head_shape(g, LANES),
                   head_shape(g, LANES, F32)],
        scratch_shapes=[pltpu.VMEM((tm, D_MODEL), BF16)],
        compiler_params=_cparams(1),
        name="nsa_proj",
    )(x, gain, w, cos, sin, gate_b)


def _swa_proj_kernel(x_ref, gain_ref, w_ref, b_ref, cos_ref, sin_ref,
                     q_ref, k_ref, v_ref, hn_ref, *, tm):
    hn_ref[...] = _rms_bf16(x_ref[...], gain_ref[...])
    cos = cos_ref[...]
    sin = sin_ref[...]
    lane = lax.broadcasted_iota(jnp.int32, (tm, MXU_N), 1)
    first_half = (lane % HEAD_DIM) < HALF
    lane1 = lax.broadcasted_iota(jnp.int32, (tm, LANES), 1)
    low = lane1 < HEAD_DIM
    zeros = jnp.zeros((tm, LANES), F32)
    ones_lane = jnp.where(lane1 >= HEAD_DIM, 1.0, 0.0)
    for c in range(4):
        sl = slice(c * MXU_N, (c + 1) * MXU_N)
        a = _dot(hn_ref[...], w_ref[:, sl]) + b_ref[:, sl]
        _store_heads(_rope(a, cos, sin, first_half) * Q_SCALE, q_ref, c * 4, zeros, low)
    sl = slice(4 * MXU_N, 5 * MXU_N)
    a = _dot(hn_ref[...], w_ref[:, sl]) + b_ref[:, sl]
    ak = _rope(a, cos, sin, first_half)
    kv = SWA_KV_HEADS * HEAD_DIM
    _store_heads(ak[:, :kv], k_ref, 0, zeros, low)
    _store_heads(a[:, kv:], v_ref, 0, ones_lane, low)


def _swa_proj(x, gain, w, b, cos, sin, *, batch, seq):
    tm = PROJ_TM
    tokens = batch * seq
    tps = seq // tm
    head_spec = lambda n: pl.BlockSpec((1, n, tm, LANES), lambda i: (i // tps, 0, i % tps, 0))
    head_shape = lambda n: jax.ShapeDtypeStruct((batch, n, seq, LANES), BF16)
    row_spec = pl.BlockSpec((tm, D_MODEL), lambda i: (i, 0))
    tab_spec = pl.BlockSpec((tm, MXU_N), lambda i: (i % tps, 0))
    return pl.pallas_call(
        functools.partial(_swa_proj_kernel, tm=tm),
        grid=(tokens // tm,),
        in_specs=[row_spec, _full(gain), _full(w), _full(b), tab_spec, tab_spec],
        out_specs=[head_spec(SWA_HEADS), head_spec(SWA_KV_HEADS), head_spec(SWA_KV_HEADS)],
        out_shape=[head_shape(SWA_HEADS), head_shape(SWA_KV_HEADS), head_shape(SWA_KV_HEADS)],
        scratch_shapes=[pltpu.VMEM((tm, D_MODEL), BF16)],
        compiler_params=_cparams(1),
        name="swa_proj",
    )(x, gain, w, b, cos, sin)


def _gelu_tanh(x):
    c = float(np.sqrt(2.0 / np.pi))
    return 0.5 * x * (1.0 + jnp.tanh(c * (x + 0.044715 * (x * x * x))))


def _compress_kernel(xk_ref, xv_ref, pe_ref, w1_ref, w2_ref, ok_ref, ov_ref, *, rows):
    half = CMP_STRIDE * HEAD_DIM
    lane = lax.broadcasted_iota(jnp.int32, (rows, LANES), 1)
    for t, (x_ref, o_ref) in enumerate(((xk_ref, ok_ref), (xv_ref, ov_ref))):
        x = x_ref[...].astype(F32)
        lo = _dot((x + pe_ref[t, 0:1]).astype(BF16), w1_ref[t, :half])
        hi = _dot((x + pe_ref[t, 1:2]).astype(BF16), w1_ref[t, half:])
        pre = lo + pltpu.roll(hi, rows - 1, 0)
        y = _dot(_gelu_tanh(pre).astype(BF16), w2_ref[t])
        if t == 1:
            y = y + jnp.where(lane >= HEAD_DIM, 1.0, 0.0)
        o_ref[...] = y.astype(BF16)


def _compress(xk, xv, pe, w1, w2, *, rows=1024):
    n = xk.shape[0]
    rows = min(rows, n)
    row_spec = lambda w: pl.BlockSpec((rows, w), lambda i: (i, 0))
    out = jax.ShapeDtypeStruct((n, LANES), BF16)
    return pl.pallas_call(
        functools.partial(_compress_kernel, rows=rows),
        grid=(n // rows,),
        in_specs=[row_spec(xk.shape[1]), row_spec(xk.shape[1]), _full(pe), _full(w1), _full(w2)],
        out_specs=[row_spec(LANES), row_spec(LANES)],
        out_shape=[out, out],
        compiler_params=_cparams(1),
        name="nsa_compress",
    )(xk, xv, pe, w1, w2)


def _lane_fold_max(s):
    fold = s[:, :LANES]
    for i in range(1, s.shape[1] // LANES):
        fold = jnp.maximum(fold, s[:, i * LANES:(i + 1) * LANES])
    return fold


def _row_max_lanes(fold):
    return jnp.broadcast_to(jnp.max(fold, axis=1, keepdims=True), fold.shape)


def _softmax_weights(s, mb):
    return jnp.concatenate([jnp.exp2(s[:, i * LANES:(i + 1) * LANES] - mb)
                            for i in range(s.shape[1] // LANES)], axis=1).astype(BF16)


def _span_attention(q, k_ref, v_ref, bias, k0, span, *, r_heads, tq, floor=None):
    rows = r_heads * tq
    s = _dot_nt(q, k_ref[0, 0, pl.ds(k0, span), :])
    s = (s.reshape(r_heads, tq, span) + bias[None]).reshape(rows, span)
    fold = _lane_fold_max(s)
    if floor is not None:
        fold = jnp.maximum(fold, floor)
    mb = _row_max_lanes(fold)
    return _dot(_softmax_weights(s, mb), v_ref[0, 0, pl.ds(k0, span), :]), mb


def _normalized_heads(acc, r_heads, tq, den=None, guard=False):
    if den is None:
        den = pltpu.roll(acc, HEAD_DIM, 1)
    if guard:
        den = jnp.where(den > 0.0, den, 1.0)
    o = acc / den
    return jnp.concatenate([o[r * tq:(r + 1) * tq, :HEAD_DIM] for r in range(r_heads)], axis=1)


def _band_bias(tq, n_tiles, window):
    row = np.arange(tq)[:, None]
    col = np.arange(n_tiles * tq)[None, :]
    cases = []
    for c in range(n_tiles):
        d = (row + c * tq) - col
        cases.append(np.where((d >= 0) & (d < window), 0.0, NEG))
    return jnp.asarray(np.stack(cases), F32)


def _nsa_attn_kernel(q_ref, kc_ref, vc_ref, ks_ref, vs_ref, kw_ref, vw_ref, gate_ref, ovl_ref, gexp_ref,
                     causal_ref, wbias_ref, o_ref, qs_ref, s_ref, mx_ref, acc_ref, osum_ref, *, tq, n_slc):
    qi = pl.program_id(2)
    rows = NSA_R * tq
    gates = gate_ref[0, 0]
    g_hi = gates.astype(BF16)
    g_lo = (gates - g_hi.astype(F32)).astype(BF16)

    def gate_lanes(branch):
        return _dot(g_hi, gexp_ref[branch]) + _dot(g_lo, gexp_ref[branch])

    def q_plain():
        return q_ref[0].reshape(rows, LANES)

    lane = lax.broadcasted_iota(jnp.int32, (tq, LANES), 1)
    qpos = qi * tq + lax.broadcasted_iota(jnp.int32, (tq, LANES), 0)
    allowed = ((lane * CMP_STRIDE + (CMP_LEN - 1)) <= qpos)[None]
    sc = _dot_nt(q_plain(), kc_ref[0, 0]).reshape(NSA_R, tq, LANES)
    sc = jnp.where(allowed, sc, NEG)
    pc = jnp.where(allowed, jnp.exp2(sc - jnp.max(sc, axis=-1, keepdims=True)), 0.0)
    pc = pc.reshape(rows, LANES).astype(BF16)
    oc = _dot(pc, vc_ref[0, 0])
    osum_ref[...] = _normalized_heads(oc, NSA_R, tq, guard=True) * gate_lanes(0)

    t = _dot_nt(ovl_ref[...], pc)
    den_t = t[n_slc:n_slc + 1, :]
    impn = t[:n_slc, :] * (1.0 / jnp.where(den_t > 0.0, den_t, 1.0))
    imp_t = impn[:, :tq]
    for r in range(1, NSA_R):
        imp_t = imp_t + impn[:, r * tq:(r + 1) * tq]
    blk = lax.broadcasted_iota(jnp.int32, (n_slc, tq), 0)
    cur = (qi * tq + lax.broadcasted_iota(jnp.int32, (n_slc, tq), 1)) // SLC_LEN
    forced = (blk == 0) | (blk == cur) | (blk == cur - 1)
    causal = blk <= cur
    score = jnp.where(causal, imp_t + jnp.where(forced, FORCE_BONUS, 0.0), -1.0)
    rank = jnp.zeros((n_slc, tq), F32)
    for m_blk in range(n_slc):
        sm = score[m_blk:m_blk + 1, :]
        rank = rank + jnp.where(blk > m_blk, jnp.where(sm >= score, 1.0, 0.0),
                                jnp.where(sm > score, 1.0, 0.0))
    selected = causal & (rank < float(SLC_TOPK))
    bias_t = jnp.concatenate([jnp.zeros((HEAD_DIM, tq), F32), jnp.where(selected, 0.0, NEG),
                              jnp.zeros((LANES - HEAD_DIM - n_slc, tq), F32)], axis=0)
    selb = bias_t.T.astype(BF16)
    qs_ref[...] = (q_ref[0] + selb[None]).reshape(rows, LANES)

    groups = [(4, 0), (2, qi & 4), (1, qi & 6)]

    def slc_scores(t0, n, diag):
        k0 = pl.multiple_of(t0 * tq, tq)
        s = _dot_nt(qs_ref[...], ks_ref[0, 0, pl.ds(k0, n * tq), :])
        if diag:
            s = (s.reshape(NSA_R, tq, tq) + causal_ref[...][None]).reshape(rows, tq)
        for i in range(n):
            s_ref[t0 + i] = s[:, i * tq:(i + 1) * tq]
        mx_ref[...] = jnp.maximum(mx_ref[...], _lane_fold_max(s))

    def slc_weighted(t0, n):
        k0 = pl.multiple_of(t0 * tq, tq)
        mb = mx_ref[...]
        p = jnp.concatenate([_softmax_weights(s_ref[t0 + i], mb) for i in range(n)], axis=1)
        acc_ref[...] += _dot(p, vs_ref[0, 0, pl.ds(k0, n * tq), :])

    mx_ref[...] = jnp.full((rows, LANES), NEG, F32)
    for n, t0 in groups:
        pl.when((qi & n) != 0)(functools.partial(slc_scores, t0, n, False))
    slc_scores(qi, 1, True)
    mx_ref[...] = _row_max_lanes(mx_ref[...])
    acc_ref[...] = jnp.zeros_like(acc_ref)
    for n, t0 in groups:
        pl.when((qi & n) != 0)(functools.partial(slc_weighted, t0, n))
    slc_weighted(qi, 1)
    osum_ref[...] += _normalized_heads(acc_ref[...], NSA_R, tq) * gate_lanes(1)

    before = jnp.minimum(qi, WIN_TILES - 1)
    k0 = pl.multiple_of((qi - before) * tq, tq)
    acc, _ = _span_attention(q_plain(), kw_ref, vw_ref, wbias_ref[before], k0,
                             WIN_TILES * tq, r_heads=NSA_R, tq=tq)
    o = osum_ref[...] + _normalized_heads(acc, NSA_R, tq) * gate_lanes(2)
    o_ref[0] = o.astype(BF16)


def _nsa_attn(q, kc, vc, ks, vs, kw, vw, gates, ovl, *, batch, seq):
    tq = NSA_TQ
    g = NSA_KV_HEADS
    rows = NSA_R * tq
    n_cmp = kc.shape[2]
    gw = NSA_R * HEAD_DIM
    gexp = np.zeros((3, LANES, gw), np.float32)
    for j in range(3):
        for r in range(NSA_R):
            gexp[j, j * NSA_R + r, r * HEAD_DIM:(r + 1) * HEAD_DIM] = 1.0
    gexp = jnp.asarray(gexp, BF16)
    causal = _band_bias(tq, 1, seq)[0]
    wbias = _band_bias(tq, WIN_TILES, WIN_LEN)
    per_bg = lambda n: pl.BlockSpec((1, 1, n, LANES), lambda b, h, i: (b, h, 0, 0))
    return pl.pallas_call(
        functools.partial(_nsa_attn_kernel, tq=tq, n_slc=seq // SLC_LEN),
        grid=(batch, g, seq // tq),
        in_specs=[pl.BlockSpec((1, NSA_R, tq, LANES), lambda b, h, i: (b, h, i, 0)),
                  per_bg(n_cmp), per_bg(n_cmp), per_bg(seq), per_bg(seq), per_bg(seq), per_bg(seq),
                  pl.BlockSpec((1, 1, tq, LANES), lambda b, h, i: (b, h, i, 0)),
                  _full(ovl), _full(gexp), _full(causal), _full(wbias)],
        out_specs=pl.BlockSpec((1, tq, gw), lambda b, h, i: (b, i, h)),
        out_shape=jax.ShapeDtypeStruct((batch, seq, NSA_HEADS * HEAD_DIM), BF16),
        scratch_shapes=[pltpu.VMEM((rows, LANES), BF16),
                        pltpu.VMEM((seq // tq, rows, tq), F32),
                        pltpu.VMEM((rows, LANES), F32),
                        pltpu.VMEM((rows, LANES), F32),
                        pltpu.VMEM((tq, gw), F32)],
        compiler_params=_cparams(3),
        name="nsa_attn",
    )(q, kc, vc, ks, vs, kw, vw, gates, ovl, gexp, causal, wbias)


def _swa_attn_kernel(q_ref, k_ref, v_ref, sink_ref, bias_ref, o_ref, *, tq):
    qi = pl.program_id(2)
    rows = SWA_R * tq
    sink = sink_ref[0] * LOG2E
    before = jnp.minimum(qi, SWA_TILES - 1)
    k0 = pl.multiple_of((qi - before) * tq, tq)
    acc, mb = _span_attention(q_ref[0].reshape(rows, LANES), k_ref, v_ref, bias_ref[before],
                              k0, SWA_TILES * tq, r_heads=SWA_R, tq=tq, floor=sink)
    den = pltpu.roll(acc, HEAD_DIM, 1) + jnp.exp2(sink - mb)
    o_ref[0] = _normalized_heads(acc, SWA_R, tq, den=den).astype(BF16)


def _swa_attn(q, k, v, sink_rows, *, batch, seq):
    tq = SWA_TQ
    rows = SWA_R * tq
    bias = _band_bias(tq, SWA_TILES, SWA_WINDOW)
    per_bg = pl.BlockSpec((1, 1, seq, LANES), lambda b, h, i: (b, h, 0, 0))
    return pl.pallas_call(
        functools.partial(_swa_attn_kernel, tq=tq),
        grid=(batch, SWA_KV_HEADS, seq // tq),
        in_specs=[pl.BlockSpec((1, SWA_R, tq, LANES), lambda b, h, i: (b, h, i, 0)),
                  per_bg, per_bg,
                  pl.BlockSpec((1, rows, LANES), lambda b, h, i: (h, 0, 0)),
                  _full(bias)],
        out_specs=pl.BlockSpec((1, tq, SWA_R * HEAD_DIM), lambda b, h, i: (b, i, h)),
        out_shape=jax.ShapeDtypeStruct((batch, seq, SWA_HEADS * HEAD_DIM), BF16),
        compiler_params=_cparams(3),
        name="swa_attn",
    )(q, k, v, sink_rows, bias)


def _out_proj_kernel(*refs, use_bias):
    o_ref, x_ref, w_ref = refs[:3]
    out_ref = refs[-1]
    y = _dot(o_ref[...], w_ref[...]) + x_ref[...]
    if use_bias:
        y = y + refs[3][...]
    out_ref[...] = y


def _out_proj(o, x, w, b=None):
    tm = PROJ_TM
    tokens = x.shape[0]
    row = pl.BlockSpec((tm, D_MODEL), lambda i: (i, 0))
    args = [o, x, w] + ([b] if b is not None else [])
    in_specs = [row, row, _full(w)] + ([_full(b)] if b is not None else [])
    return pl.pallas_call(
        functools.partial(_out_proj_kernel, use_bias=b is not None),
        grid=(tokens // tm,),
        in_specs=in_specs,
        out_specs=row,
        out_shape=jax.ShapeDtypeStruct((tokens, D_MODEL), F32),
        compiler_params=_cparams(1),
        name="out_proj",
    )(*args)


def _ffn_kernel(x_ref, halo_ref, gain_ref, wg_ref, wu_ref, cw_ref, cb_ref, wd_ref, fin_ref,
                o_ref, hn_ref, act_ref, acc_ref, *, tm, tiles_per_seq, final_norm):
    gain = gain_ref[...]
    hn_ref[0:HALO] = _rms_bf16(halo_ref[...], gain)
    hn_ref[HALO:] = _rms_bf16(x_ref[...], gain)
    seq_start = (pl.program_id(0) % tiles_per_seq) == 0
    row = lax.broadcasted_iota(jnp.int32, (tm + HALO, FF_CHUNK), 0)
    keep = jnp.logical_not(seq_start & (row < HALO))

    def up(c):
        gpre = _dot(hn_ref[...], wg_ref[c])
        u = _dot(hn_ref[HALO:], wu_ref[c])
        gpre = jnp.where(keep, gpre, 0.0)
        cw = cw_ref[c]
        y = (cw[2:3] * gpre[HALO:]
             + cw[1:2] * pltpu.roll(gpre, 1, 0)[HALO:]
             + cw[0:1] * pltpu.roll(gpre, 2, 0)[HALO:]
             + cb_ref[c])
        return (y * (1.0 / (1.0 + jnp.exp(-y))) * u).astype(BF16)

    act_ref[...] = up(0)
    acc_ref[...] = jnp.zeros_like(acc_ref)

    def body(c, carry):
        act = act_ref[...]
        acc_ref[...] += _dot(act, wd_ref[c])
        act_ref[...] = up(c + 1)
        return carry

    lax.fori_loop(0, N_FF_CHUNKS - 1, body, 0)
    y = x_ref[...] + acc_ref[...] + _dot(act_ref[...], wd_ref[N_FF_CHUNKS - 1])
    if final_norm:
        ms = jnp.mean(y * y, axis=-1, keepdims=True)
        y = y * lax.rsqrt(ms + NORM_EPS) * fin_ref[...]
    o_ref[...] = y


def _ffn(x, gain, wg, wu, cw, cb, wd, fin_gain, *, seq, final_norm):
    tm = PROJ_TM
    tokens = x.shape[0]
    tps = seq // tm
    hb = tm // HALO
    row = pl.BlockSpec((tm, D_MODEL), lambda i: (i, 0))
    halo = pl.BlockSpec((HALO, D_MODEL), lambda i: (jnp.maximum(i * hb - 1, 0), 0))
    return pl.pallas_call(
        functools.partial(_ffn_kernel, tm=tm, tiles_per_seq=tps, final_norm=final_norm),
        grid=(tokens // tm,),
        in_specs=[row, halo, _full(gain), _full(wg), _full(wu), _full(cw), _full(cb), _full(wd),
                  _full(fin_gain)],
        out_specs=row,
        out_shape=jax.ShapeDtypeStruct((tokens, D_MODEL), F32),
        scratch_shapes=[pltpu.VMEM((tm + HALO, D_MODEL), BF16), pltpu.VMEM((tm, FF_CHUNK), BF16),
                        pltpu.VMEM((tm, D_MODEL), F32)],
        compiler_params=_cparams(1),
        name="conv_ffn",
    )(x, x, gain, wg, wu, cw, cb, wd, fin_gain)


def _rope_tables(seq):
    pos = jnp.arange(seq, dtype=F32)
    inv = ROPE_THETA ** (-jnp.arange(0, HEAD_DIM, 2, dtype=F32) / HEAD_DIM)
    ang = pos[:, None] * inv[None, :]
    cos, sin = jnp.cos(ang), jnp.sin(ang)
    reps = MXU_N // HEAD_DIM
    cos_t = jnp.tile(jnp.concatenate([cos, cos], axis=1), (1, reps))
    sin_t = jnp.tile(jnp.concatenate([-sin, sin], axis=1), (1, reps))
    return cos_t, sin_t


def _overlap_rows(seq):
    n_cmp = (seq - CMP_LEN) // CMP_STRIDE + 1
    n_slc = seq // SLC_LEN
    cstart = np.arange(LANES) * CMP_STRIDE
    sstart = np.arange(LANES) * SLC_LEN
    ovl = ((cstart[None, :] <= sstart[:, None] + SLC_LEN - 1)
           & (cstart[None, :] + CMP_LEN - 1 >= sstart[:, None])
           & (np.arange(LANES)[None, :] < n_cmp) & (np.arange(LANES)[:, None] < n_slc))
    ovl[n_slc, :] = True
    return jnp.asarray(ovl, dtype=BF16)


def _nsa_layer(x, gain, cos, sin, w_in, cmp_pe, cmp_w1, cmp_w2, gate_b, w_o, ovl, *, batch, seq):
    n_qkv = NSA_HEADS * HEAD_DIM + 6 * NSA_KV_HEADS * HEAD_DIM
    perm = np.array([(g * NSA_R + r) * 3 + j for g in range(NSA_KV_HEADS) for j in range(3)
                     for r in range(NSA_R)])
    pad = LANES - 3 * NSA_HEADS
    w = jnp.concatenate([w_in[:, :n_qkv], w_in[:, n_qkv:][:, perm],
                         jnp.zeros((D_MODEL, pad), F32)], axis=1).astype(BF16)
    gb = jnp.concatenate([gate_b[perm], jnp.zeros((pad,), F32)])[None, :]
    q, kc, vc, ks, vs, kw, vw, gates = _nsa_proj(x, gain[None, :], w, cos, sin, gb, batch=batch, seq=seq)

    rows16 = seq // CMP_STRIDE
    flat = lambda t: t.reshape(batch * NSA_KV_HEADS * rows16, CMP_STRIDE * HEAD_DIM)
    pe = cmp_pe.reshape(2, 2, CMP_STRIDE * HEAD_DIM)
    w2 = jnp.pad(cmp_w2, ((0, 0), (0, 0), (0, LANES - HEAD_DIM))).astype(BF16)
    kcc, vcc = _compress(flat(kc), flat(vc), pe, cmp_w1.astype(BF16), w2)
    kcc = kcc.reshape(batch, NSA_KV_HEADS, rows16, LANES)
    vcc = vcc.reshape(batch, NSA_KV_HEADS, rows16, LANES)

    o = _nsa_attn(q, kcc, vcc, ks, vs, kw, vw, gates, ovl, batch=batch, seq=seq)
    return _out_proj(o.reshape(batch * seq, NSA_HEADS * HEAD_DIM), x, w_o.astype(BF16))


def _swa_layer(x, gain, cos, sin, w_qkv, b_qkv, sinks, w_o, b_o, *, batch, seq):
    q, k, v = _swa_proj(x, gain[None, :], w_qkv.astype(BF16), b_qkv[None, :], cos, sin,
                        batch=batch, seq=seq)
    sink_rows = jnp.broadcast_to(
        jnp.repeat(sinks.reshape(SWA_KV_HEADS, SWA_R), SWA_TQ, axis=1)[:, :, None],
        (SWA_KV_HEADS, SWA_R * SWA_TQ, LANES))
    o = _swa_attn(q, k, v, sink_rows, batch=batch, seq=seq)
    return _out_proj(o.reshape(batch * seq, SWA_HEADS * HEAD_DIM), x, w_o.astype(BF16), b_o[None, :])


def _ffn_layer(x, gain, w_gu, conv_w, conv_b, w_down, fin_gain, *, seq, final_norm):
    chunks = lambda w: w.reshape(D_MODEL, N_FF_CHUNKS, FF_CHUNK).transpose(1, 0, 2).astype(BF16)
    wg = chunks(w_gu[:, :D_FF])
    wu = chunks(w_gu[:, D_FF:])
    wd = w_down.reshape(N_FF_CHUNKS, FF_CHUNK, D_MODEL).astype(BF16)
    cw = conv_w.reshape(CONV_WIDTH, N_FF_CHUNKS, FF_CHUNK).transpose(1, 0, 2)
    cb = conv_b.reshape(N_FF_CHUNKS, 1, FF_CHUNK)
    return _ffn(x, gain[None, :], wg, wu, cw, cb, wd, fin_gain[None, :], seq=seq, final_norm=final_norm)


def kernel(x, nsa_w_in, nsa_cmp_pe, nsa_cmp_w1, nsa_cmp_w2, nsa_gate_b, nsa_w_o,
           swa_w_qkv, swa_b_qkv, swa_sinks, swa_w_o, swa_b_o,
           ffn_w_gu, ffn_conv_w, ffn_conv_b, ffn_w_down,
           norm_mix, norm_ffn, norm_final):
    batch, seq, _ = x.shape
    depth = ffn_w_gu.shape[0]
    cos, sin = _rope_tables(seq)
    ovl = _overlap_rows(seq)
    h = x.reshape(batch * seq, D_MODEL)
    for i in range(depth):
        j = i // 2
        if i % 2 == 0:
            h = _nsa_layer(h, norm_mix[i], cos, sin, nsa_w_in[j], nsa_cmp_pe[j], nsa_cmp_w1[j],
                           nsa_cmp_w2[j], nsa_gate_b[j], nsa_w_o[j], ovl, batch=batch, seq=seq)
        else:
            h = _swa_layer(h, norm_mix[i], cos, sin, swa_w_qkv[j], swa_b_qkv[j], swa_sinks[j],
                           swa_w_o[j], swa_b_o[j], batch=batch, seq=seq)
        h = _ffn_layer(h, norm_ffn[i], ffn_w_gu[i], ffn_conv_w[i], ffn_conv_b[i], ffn_w_down[i],
                       norm_final, seq=seq, final_norm=(i == depth - 1))
    return h.reshape(batch, seq, D_MODEL)
```

```python
import functools

import numpy as np
import jax
import jax.numpy as jnp
from jax import lax
from jax.experimental import pallas as pl
from jax.experimental.pallas import tpu as pltpu

F32 = jnp.float32
BF16 = jnp.bfloat16

D_MODEL = 1024
HEAD_DIM = 64
HALF = HEAD_DIM // 2
ROPE_THETA = 10000.0
NORM_EPS = 1e-6

NSA_HEADS = 16
NSA_KV_HEADS = 4
NSA_R = NSA_HEADS // NSA_KV_HEADS
CMP_LEN = 32
CMP_STRIDE = 16
CMP_HIDDEN = 256
SLC_LEN = 64
SLC_TOPK = 16
WIN_LEN = 512
FORCE_BONUS = 1e4

SWA_HEADS = 16
SWA_KV_HEADS = 2
SWA_R = SWA_HEADS // SWA_KV_HEADS
SWA_WINDOW = 128

D_FF = 2816
CONV_WIDTH = 3

LANES = 128
MXU_N = 256
NEG = -1e30
LOG2E = float(np.log2(np.e))
Q_SCALE = HEAD_DIM ** -0.5 * LOG2E
VMEM_LIMIT = 56 * 1024 * 1024

FF_CHUNK = MXU_N
N_FF_CHUNKS = D_FF // FF_CHUNK
HALO = 16

PROJ_TM = 512
NSA_TQ = 256
WIN_TILES = WIN_LEN // NSA_TQ + 1
NSA_GB = 2
SWA_TQ = 128
SWA_TILES = SWA_WINDOW // SWA_TQ + 1


def _cparams(n_axes):
    return pltpu.CompilerParams(dimension_semantics=("arbitrary",) * n_axes,
                                vmem_limit_bytes=VMEM_LIMIT)


def _full(a):
    return pl.BlockSpec(a.shape, lambda *_: (0,) * a.ndim)


def _rms_bf16(x, gain):
    ms = jnp.mean(x * x, axis=-1, keepdims=True)
    return (x * lax.rsqrt(ms + NORM_EPS) * gain).astype(BF16)


def _dot(a, b):
    return jnp.dot(a, b, preferred_element_type=F32)


def _dot_nt(a, b):
    return lax.dot_general(a, b, (((1,), (1,)), ((), ())), preferred_element_type=F32)


def _rope(a, cos, sin_signed, first_half):
    n = a.shape[1]
    rot = jnp.where(first_half, pltpu.roll(a, n - HALF, 1), pltpu.roll(a, HALF, 1))
    return a * cos + rot * sin_signed


def _store_heads(a, ref, base, extra, low):
    for j in range(a.shape[1] // HEAD_DIM):
        slab = a[:, (j // 2) * LANES:(j // 2 + 1) * LANES]
        if j % 2:
            slab = pltpu.roll(slab, HEAD_DIM, 1)
        ref[0, base + j] = jnp.where(low, slab, extra).astype(BF16)


def _nsa_proj_kernel(x_ref, gain_ref, w_ref, cos_ref, sin_ref, gb_ref,
                     q_ref, kc_ref, vc_ref, ks_ref, vs_ref, kw_ref, vw_ref, gate_ref,
                     hn_ref, tok_ref, *, tm, tiles_per_seq):
    hn_ref[...] = _rms_bf16(x_ref[...], gain_ref[...])
    cos = cos_ref[...]
    sin = sin_ref[...]
    lane = lax.broadcasted_iota(jnp.int32, (tm, MXU_N), 1)
    first_half = (lane % HEAD_DIM) < HALF
    lane1 = lax.broadcasted_iota(jnp.int32, (tm, LANES), 1)
    low = lane1 < HEAD_DIM
    zeros = jnp.zeros((tm, LANES), F32)
    ones_lane = jnp.where(lane1 >= HEAD_DIM, 1.0, 0.0)
    t = pl.program_id(0) % tiles_per_seq
    blk = (t * tm + lax.broadcasted_iota(jnp.int32, (tm, LANES), 0)) // SLC_LEN
    onehot = jnp.where(lane1 - HEAD_DIM == blk, 1.0, 0.0)

    def chunk(c):
        return _dot(hn_ref[...], w_ref[:, c * MXU_N:(c + 1) * MXU_N])

    def block_rows(a, ref):
        nrow = tm // CMP_STRIDE
        low_r = lax.broadcasted_iota(jnp.int32, (nrow, LANES), 1) < HEAD_DIM
        for s in range(NSA_KV_HEADS // 2):
            tok_ref[s] = a[:, s * LANES:(s + 1) * LANES]
        for s in range(NSA_KV_HEADS // 2):
            taps = [tok_ref[s, pl.ds(l, nrow, stride=CMP_STRIDE), :] for l in range(CMP_STRIDE)]
            even, odd = [], []
            for m in range(CMP_STRIDE // 2):
                a0 = taps[2 * m]
                a1 = taps[2 * m + 1]
                even.append(jnp.where(low_r, a0, pltpu.roll(a1, HEAD_DIM, 1)))
                odd.append(jnp.where(low_r, pltpu.roll(a0, HEAD_DIM, 1), a1))
            ref[0, 2 * s] = jnp.concatenate(even, axis=1).astype(BF16)
            ref[0, 2 * s + 1] = jnp.concatenate(odd, axis=1).astype(BF16)

    for c in range(4):
        a = _rope(chunk(c), cos, sin, first_half) * Q_SCALE
        _store_heads(a, q_ref, c * NSA_R, zeros, low)
    block_rows(_rope(chunk(4), cos, sin, first_half), kc_ref)
    block_rows(chunk(5), vc_ref)
    _store_heads(_rope(chunk(6), cos, sin, first_half), ks_ref, 0, onehot, low)
    _store_heads(chunk(7), vs_ref, 0, ones_lane, low)
    _store_heads(_rope(chunk(8), cos, sin, first_half), kw_ref, 0, zeros, low)
    _store_heads(chunk(9), vw_ref, 0, ones_lane, low)
    z = _dot(hn_ref[...], w_ref[:, 10 * MXU_N:10 * MXU_N + LANES]) + gb_ref[...]
    sig = 1.0 / (1.0 + jnp.exp(-z))
    per_group = 3 * NSA_R
    for g in range(NSA_KV_HEADS):
        gate_ref[0, g] = sig if g == 0 else pltpu.roll(sig, LANES - per_group * g, 1)


def _nsa_proj(x, gain, w, cos, sin, gate_b, *, batch, seq):
    tm = PROJ_TM
    tokens = batch * seq
    tps = seq // tm
    g = NSA_KV_HEADS
    head_spec = lambda n, w_: pl.BlockSpec((1, n, tm, w_), lambda i: (i // tps, 0, i % tps, 0))
    head_shape = lambda n, w_, dt=BF16: jax.ShapeDtypeStruct((batch, n, seq, w_), dt)
    row_spec = pl.BlockSpec((tm, D_MODEL), lambda i: (i, 0))
    tab_spec = pl.BlockSpec((tm, MXU_N), lambda i: (i % tps, 0))
    flat_w = CMP_STRIDE * HEAD_DIM
    flat_spec = pl.BlockSpec((1, g, tm // CMP_STRIDE, flat_w), lambda i: (i // tps, 0, i % tps, 0))
    flat_shape = jax.ShapeDtypeStruct((batch, g, seq // CMP_STRIDE, flat_w), BF16)
    return pl.pallas_call(
        functools.partial(_nsa_proj_kernel, tm=tm, tiles_per_seq=tps),
        grid=(tokens // tm,),
        in_specs=[row_spec, _full(gain), _full(w), tab_spec, tab_spec, _full(gate_b)],
        out_specs=[head_spec(NSA_HEADS, LANES), flat_spec, flat_spec,
                   head_spec(g, LANES), head_spec(g, LANES), head_spec(g, LANES), head_spec(g, LANES),
                   head_spec(g, LANES)],
        out_shape=[head_shape(NSA_HEADS, LANES), flat_shape, flat_shape,
                   head_shape(g, LANES), head_shape(g, LANES), head_shape(g, LANES), head_shape(g, LANES),
                   head_shape(g, LANES, F32)],
        scratch_shapes=[pltpu.VMEM((tm, D_MODEL), BF16), pltpu.VMEM((g // 2, tm, LANES), F32)],
        compiler_params=_cparams(1),
        name="nsa_proj",
    )(x, gain, w, cos, sin, gate_b)


def _swa_proj_kernel(x_ref, gain_ref, w_ref, b_ref, cos_ref, sin_ref,
                     q_ref, k_ref, v_ref, hn_ref, *, tm):
    hn_ref[...] = _rms_bf16(x_ref[...], gain_ref[...])
    cos = cos_ref[...]
    sin = sin_ref[...]
    lane = lax.broadcasted_iota(jnp.int32, (tm, MXU_N), 1)
    first_half = (lane % HEAD_DIM) < HALF
    lane1 = lax.broadcasted_iota(jnp.int32, (tm, LANES), 1)
    low = lane1 < HEAD_DIM
    zeros = jnp.zeros((tm, LANES), F32)
    ones_lane = jnp.where(lane1 >= HEAD_DIM, 1.0, 0.0)
    for c in range(4):
        sl = slice(c * MXU_N, (c + 1) * MXU_N)
        a = _dot(hn_ref[...], w_ref[:, sl]) + b_ref[:, sl]
        _store_heads(_rope(a, cos, sin, first_half) * Q_SCALE, q_ref, c * 4, zeros, low)
    sl = slice(4 * MXU_N, 5 * MXU_N)
    a = _dot(hn_ref[...], w_ref[:, sl]) + b_ref[:, sl]
    ak = _rope(a, cos, sin, first_half)
    kv = SWA_KV_HEADS * HEAD_DIM
    _store_heads(ak[:, :kv], k_ref, 0, zeros, low)
    _store_heads(a[:, kv:], v_ref, 0, ones_lane, low)


def _swa_proj(x, gain, w, b, cos, sin, *, batch, seq):
    tm = PROJ_TM
    tokens = batch * seq
    tps = seq // tm
    head_spec = lambda n: pl.BlockSpec((1, n, tm, LANES), lambda i: (i // tps, 0, i % tps, 0))
    head_shape = lambda n: jax.ShapeDtypeStruct((batch, n, seq, LANES), BF16)
    row_spec = pl.BlockSpec((tm, D_MODEL), lambda i: (i, 0))
    tab_spec = pl.BlockSpec((tm, MXU_N), lambda i: (i % tps, 0))
    return pl.pallas_call(
        functools.partial(_swa_proj_kernel, tm=tm),
        grid=(tokens // tm,),
        in_specs=[row_spec, _full(gain), _full(w), _full(b), tab_spec, tab_spec],
        out_specs=[head_spec(SWA_HEADS), head_spec(SWA_KV_HEADS), head_spec(SWA_KV_HEADS)],
        out_shape=[head_shape(SWA_HEADS), head_shape(SWA_KV_HEADS), head_shape(SWA_KV_HEADS)],
        scratch_shapes=[pltpu.VMEM((tm, D_MODEL), BF16)],
        compiler_params=_cparams(1),
        name="swa_proj",
    )(x, gain, w, b, cos, sin)


def _gelu_tanh(x):
    c = float(np.sqrt(2.0 / np.pi))
    return 0.5 * x * (1.0 + jnp.tanh(c * (x + 0.044715 * (x * x * x))))


def _compress_kernel(xk_ref, xv_ref, pe_ref, w1_ref, w2_ref, ok_ref, ov_ref, *, rows):
    half = CMP_STRIDE * HEAD_DIM
    lane = lax.broadcasted_iota(jnp.int32, (rows, LANES), 1)
    for t, (x_ref, o_ref) in enumerate(((xk_ref, ok_ref), (xv_ref, ov_ref))):
        x = x_ref[...].astype(F32)
        lo = _dot((x + pe_ref[t, 0:1]).astype(BF16), w1_ref[t, :half])
        hi = _dot((x + pe_ref[t, 1:2]).astype(BF16), w1_ref[t, half:])
        pre = lo + pltpu.roll(hi, rows - 1, 0)
        y = _dot(_gelu_tanh(pre).astype(BF16), w2_ref[t])
        if t == 1:
            y = y + jnp.where(lane >= HEAD_DIM, 1.0, 0.0)
        o_ref[...] = y.astype(BF16)


def _compress(xk, xv, pe, w1, w2, *, rows=1024):
    n = xk.shape[0]
    rows = min(rows, n)
    row_spec = lambda w: pl.BlockSpec((rows, w), lambda i: (i, 0))
    out = jax.ShapeDtypeStruct((n, LANES), BF16)
    return pl.pallas_call(
        functools.partial(_compress_kernel, rows=rows),
        grid=(n // rows,),
        in_specs=[row_spec(xk.shape[1]), row_spec(xk.shape[1]), _full(pe), _full(w1), _full(w2)],
        out_specs=[row_spec(LANES), row_spec(LANES)],
        out_shape=[out, out],
        compiler_params=_cparams(1),
        name="nsa_compress",
    )(xk, xv, pe, w1, w2)


def _lane_fold_max(s):
    fold = s[:, :LANES]
    for i in range(1, s.shape[1] // LANES):
        fold = jnp.maximum(fold, s[:, i * LANES:(i + 1) * LANES])
    return fold


def _row_max_lanes(fold):
    return jnp.broadcast_to(jnp.max(fold, axis=1, keepdims=True), fold.shape)


def _softmax_weights(s, mb):
    return jnp.concatenate([jnp.exp2(s[:, i * LANES:(i + 1) * LANES] - mb)
                            for i in range(s.shape[1] // LANES)], axis=1).astype(BF16)


def _span_attention(q, k_ref, v_ref, bias, k0, span, *, r_heads, tq, floor=None, gi=0):
    rows = r_heads * tq
    s = _dot_nt(q, k_ref[0, gi, pl.ds(k0, span), :])
    s = (s.reshape(r_heads, tq, span) + bias[None]).reshape(rows, span)
    fold = _lane_fold_max(s)
    if floor is not None:
        fold = jnp.maximum(fold, floor)
    mb = _row_max_lanes(fold)
    return _dot(_softmax_weights(s, mb), v_ref[0, gi, pl.ds(k0, span), :]), mb


def _normalized_heads(acc, r_heads, tq, den=None, guard=False):
    if den is None:
        den = pltpu.roll(acc, HEAD_DIM, 1)
    if guard:
        den = jnp.where(den > 0.0, den, 1.0)
    o = acc / den
    return jnp.concatenate([o[r * tq:(r + 1) * tq, :HEAD_DIM] for r in range(r_heads)], axis=1)


def _band_bias(tq, n_tiles, window):
    row = np.arange(tq)[:, None]
    col = np.arange(n_tiles * tq)[None, :]
    cases = []
    for c in range(n_tiles):
        d = (row + c * tq) - col
        cases.append(np.where((d >= 0) & (d < window), 0.0, NEG))
    return jnp.asarray(np.stack(cases), F32)


def _nsa_attn_kernel(q_ref, kc_ref, vc_ref, ks_ref, vs_ref, kw_ref, vw_ref, gate_ref, ovl_ref, gexp_ref,
                     causal_ref, wbias_ref, o_ref, qs_ref, s_ref, mx_ref, acc_ref, osum_ref, *, tq, n_slc):
    qi = pl.program_id(2)
    rows = NSA_R * tq
    gw = NSA_R * HEAD_DIM
    group_ids = range(NSA_GB)

    def q_plain(gi):
        return q_ref[0, gi * NSA_R:(gi + 1) * NSA_R].reshape(rows, LANES)

    def gate_lanes(gi, branch):
        gates = gate_ref[0, gi]
        g_hi = gates.astype(BF16)
        g_lo = (gates - g_hi.astype(F32)).astype(BF16)
        return _dot(g_hi, gexp_ref[branch]) + _dot(g_lo, gexp_ref[branch])

    before = jnp.minimum(qi, WIN_TILES - 1)
    w0 = pl.multiple_of((qi - before) * tq, tq)
    for gi in group_ids:
        acc, _ = _span_attention(q_plain(gi), kw_ref, vw_ref, wbias_ref[before], w0, WIN_TILES * tq,
                                 r_heads=NSA_R, tq=tq, gi=gi)
        osum_ref[gi] = _normalized_heads(acc, NSA_R, tq) * gate_lanes(gi, 2)

    lane = lax.broadcasted_iota(jnp.int32, (tq, LANES), 1)
    qpos = qi * tq + lax.broadcasted_iota(jnp.int32, (tq, LANES), 0)
    allowed = ((lane * CMP_STRIDE + (CMP_LEN - 1)) <= qpos)[None]
    blk = lax.broadcasted_iota(jnp.int32, (n_slc, tq), 0)
    cur = (qi * tq + lax.broadcasted_iota(jnp.int32, (n_slc, tq), 1)) // SLC_LEN
    forced = (blk == 0) | (blk == cur) | (blk == cur - 1)
    causal = blk <= cur
    for gi in group_ids:
        sc = _dot_nt(q_plain(gi), kc_ref[0, gi]).reshape(NSA_R, tq, LANES)
        sc = jnp.where(allowed, sc, NEG)
        pc = jnp.where(allowed, jnp.exp2(sc - jnp.max(sc, axis=-1, keepdims=True)), 0.0)
        pc = pc.reshape(rows, LANES).astype(BF16)
        oc = _dot(pc, vc_ref[0, gi])
        osum_ref[gi] += _normalized_heads(oc, NSA_R, tq, guard=True) * gate_lanes(gi, 0)

        t = _dot_nt(ovl_ref[...], pc)
        den_t = t[n_slc:n_slc + 1, :]
        impn = t[:n_slc, :] * (1.0 / jnp.where(den_t > 0.0, den_t, 1.0))
        imp_t = impn[:, :tq]
        for r in range(1, NSA_R):
            imp_t = imp_t + impn[:, r * tq:(r + 1) * tq]
        score = jnp.where(causal, imp_t + jnp.where(forced, FORCE_BONUS, 0.0), -1.0)
        rank = jnp.zeros((n_slc, tq), F32)
        for m_blk in range(n_slc):
            sm = score[m_blk:m_blk + 1, :]
            rank = rank + jnp.where(blk > m_blk, jnp.where(sm >= score, 1.0, 0.0),
                                    jnp.where(sm > score, 1.0, 0.0))
        selected = causal & (rank < float(SLC_TOPK))
        bias_t = jnp.concatenate([jnp.zeros((HEAD_DIM, tq), F32), jnp.where(selected, 0.0, NEG),
                                  jnp.zeros((LANES - HEAD_DIM - n_slc, tq), F32)], axis=0)
        selb = bias_t.T.astype(BF16)
        qs_ref[gi] = (q_ref[0, gi * NSA_R:(gi + 1) * NSA_R] + selb[None]).reshape(rows, LANES)

    tile_groups = [(4, 0), (2, qi & 4), (1, qi & 6)]

    def slc_scores(t0, n, diag):
        k0 = pl.multiple_of(t0 * tq, tq)
        for gi in group_ids:
            s = _dot_nt(qs_ref[gi], ks_ref[0, gi, pl.ds(k0, n * tq), :])
            if diag:
                s = (s.reshape(NSA_R, tq, tq) + causal_ref[...][None]).reshape(rows, tq)
            for i in range(n):
                s_ref[gi, t0 + i] = s[:, i * tq:(i + 1) * tq]
            mx_ref[gi] = jnp.maximum(mx_ref[gi], _lane_fold_max(s))

    def slc_weighted(t0, n):
        k0 = pl.multiple_of(t0 * tq, tq)
        for gi in group_ids:
            mb = mx_ref[gi]
            p = jnp.concatenate([_softmax_weights(s_ref[gi, t0 + i], mb) for i in range(n)], axis=1)
            acc_ref[gi] += _dot(p, vs_ref[0, gi, pl.ds(k0, n * tq), :])

    mx_ref[...] = jnp.full(mx_ref.shape, NEG, F32)
    for n, t0 in tile_groups:
        pl.when((qi & n) != 0)(functools.partial(slc_scores, t0, n, False))
    slc_scores(qi, 1, True)
    for gi in group_ids:
        mx_ref[gi] = _row_max_lanes(mx_ref[gi])
    acc_ref[...] = jnp.zeros_like(acc_ref)
    for n, t0 in tile_groups:
        pl.when((qi & n) != 0)(functools.partial(slc_weighted, t0, n))
    slc_weighted(qi, 1)
    for gi in group_ids:
        o = osum_ref[gi] + _normalized_heads(acc_ref[gi], NSA_R, tq) * gate_lanes(gi, 1)
        o_ref[0, :, gi * gw:(gi + 1) * gw] = o.astype(BF16)


def _nsa_attn(q, kc, vc, ks, vs, kw, vw, gates, ovl, *, batch, seq):
    tq = NSA_TQ
    gb = NSA_GB
    rows = NSA_R * tq
    n_cmp = kc.shape[2]
    gw = NSA_R * HEAD_DIM
    gexp = np.zeros((3, LANES, gw), np.float32)
    for j in range(3):
        for r in range(NSA_R):
            gexp[j, j * NSA_R + r, r * HEAD_DIM:(r + 1) * HEAD_DIM] = 1.0
    gexp = jnp.asarray(gexp, BF16)
    causal = _band_bias(tq, 1, seq)[0]
    wbias = _band_bias(tq, WIN_TILES, WIN_LEN)
    per_bg = lambda n: pl.BlockSpec((1, gb, n, LANES), lambda b, h, i: (b, h, 0, 0))
    return pl.pallas_call(
        functools.partial(_nsa_attn_kernel, tq=tq, n_slc=seq // SLC_LEN),
        grid=(batch, NSA_KV_HEADS // gb, seq // tq),
        in_specs=[pl.BlockSpec((1, gb * NSA_R, tq, LANES), lambda b, h, i: (b, h, i, 0)),
                  per_bg(n_cmp), per_bg(n_cmp), per_bg(seq), per_bg(seq), per_bg(seq), per_bg(seq),
                  pl.BlockSpec((1, gb, tq, LANES), lambda b, h, i: (b, h, i, 0)),
                  _full(ovl), _full(gexp), _full(causal), _full(wbias)],
        out_specs=pl.BlockSpec((1, tq, gb * gw), lambda b, h, i: (b, i, h)),
        out_shape=jax.ShapeDtypeStruct((batch, seq, NSA_HEADS * HEAD_DIM), BF16),
        scratch_shapes=[pltpu.VMEM((gb, rows, LANES), BF16),
                        pltpu.VMEM((gb, seq // tq, rows, tq), F32),
                        pltpu.VMEM((gb, rows, LANES), F32),
                        pltpu.VMEM((gb, rows, LANES), F32),
                        pltpu.VMEM((gb, tq, gw), F32)],
        compiler_params=_cparams(3),
        name="nsa_attn",
    )(q, kc, vc, ks, vs, kw, vw, gates, ovl, gexp, causal, wbias)


def _swa_attn_kernel(q_ref, k_ref, v_ref, sink_ref, bias_ref, o_ref, *, tq):
    qi = pl.program_id(2)
    rows = SWA_R * tq
    sink = sink_ref[0] * LOG2E
    before = jnp.minimum(qi, SWA_TILES - 1)
    k0 = pl.multiple_of((qi - before) * tq, tq)
    acc, mb = _span_attention(q_ref[0].reshape(rows, LANES), k_ref, v_ref, bias_ref[before],
                              k0, SWA_TILES * tq, r_heads=SWA_R, tq=tq, floor=sink)
    den = pltpu.roll(acc, HEAD_DIM, 1) + jnp.exp2(sink - mb)
    o_ref[0] = _normalized_heads(acc, SWA_R, tq, den=den).astype(BF16)


def _swa_attn(q, k, v, sink_rows, *, batch, seq):
    tq = SWA_TQ
    rows = SWA_R * tq
    bias = _band_bias(tq, SWA_TILES, SWA_WINDOW)
    per_bg = pl.BlockSpec((1, 1, seq, LANES), lambda b, h, i: (b, h, 0, 0))
    return pl.pallas_call(
        functools.partial(_swa_attn_kernel, tq=tq),
        grid=(batch, SWA_KV_HEADS, seq // tq),
        in_specs=[pl.BlockSpec((1, SWA_R, tq, LANES), lambda b, h, i: (b, h, i, 0)),
                  per_bg, per_bg,
                  pl.BlockSpec((1, rows, LANES), lambda b, h, i: (h, 0, 0)),
                  _full(bias)],
        out_specs=pl.BlockSpec((1, tq, SWA_R * HEAD_DIM), lambda b, h, i: (b, i, h)),
        out_shape=jax.ShapeDtypeStruct((batch, seq, SWA_HEADS * HEAD_DIM), BF16),
        compiler_params=_cparams(3),
        name="swa_attn",
    )(q, k, v, sink_rows, bias)


def _out_proj_kernel(*refs, use_bias):
    o_ref, x_ref, w_ref = refs[:3]
    out_ref = refs[-1]
    y = _dot(o_ref[...], w_ref[...]) + x_ref[...]
    if use_bias:
        y = y + refs[3][...]
    out_ref[...] = y


def _out_proj(o, x, w, b=None):
    tm = PROJ_TM
    tokens = x.shape[0]
    row = pl.BlockSpec((tm, D_MODEL), lambda i: (i, 0))
    args = [o, x, w] + ([b] if b is not None else [])
    in_specs = [row, row, _full(w)] + ([_full(b)] if b is not None else [])
    return pl.pallas_call(
        functools.partial(_out_proj_kernel, use_bias=b is not None),
        grid=(tokens // tm,),
        in_specs=in_specs,
        out_specs=row,
        out_shape=jax.ShapeDtypeStruct((tokens, D_MODEL), F32),
        compiler_params=_cparams(1),
        name="out_proj",
    )(*args)


def _ffn_kernel(x_ref, halo_ref, gain_ref, wg_ref, wu_ref, cw_ref, cb_ref, wd_ref, fin_ref,
                o_ref, hn_ref, acc_ref, *, tm, tiles_per_seq, final_norm):
    gain = gain_ref[...]
    seq_start = (pl.program_id(0) % tiles_per_seq) == 0
    halo = _rms_bf16(halo_ref[...], gain)
    hn_ref[0:HALO] = jnp.where(seq_start, jnp.zeros_like(halo), halo)
    hn_ref[HALO:] = _rms_bf16(x_ref[...], gain)

    for c in range(N_FF_CHUNKS):
        gpre = _dot(hn_ref[...], wg_ref[c])
        u = _dot(hn_ref[HALO:], wu_ref[c])
        cw = cw_ref[c]
        y = (cw[2:3] * gpre[HALO:]
             + cw[1:2] * pltpu.roll(gpre, 1, 0)[HALO:]
             + cw[0:1] * pltpu.roll(gpre, 2, 0)[HALO:]
             + cb_ref[c])
        act = (y * (1.0 / (1.0 + jnp.exp(-y))) * u).astype(BF16)
        down = _dot(act, wd_ref[c])
        if c == 0:
            acc_ref[...] = down
        else:
            acc_ref[...] += down
    y = x_ref[...] + acc_ref[...]
    if final_norm:
        ms = jnp.mean(y * y, axis=-1, keepdims=True)
        y = y * lax.rsqrt(ms + NORM_EPS) * fin_ref[...]
    o_ref[...] = y


def _ffn(x, gain, wg, wu, cw, cb, wd, fin_gain, *, seq, final_norm):
    tm = PROJ_TM
    tokens = x.shape[0]
    tps = seq // tm
    hb = tm // HALO
    row = pl.BlockSpec((tm, D_MODEL), lambda i: (i, 0))
    halo = pl.BlockSpec((HALO, D_MODEL), lambda i: (jnp.maximum(i * hb - 1, 0), 0))
    return pl.pallas_call(
        functools.partial(_ffn_kernel, tm=tm, tiles_per_seq=tps, final_norm=final_norm),
        grid=(tokens // tm,),
        in_specs=[row, halo, _full(gain), _full(wg), _full(wu), _full(cw), _full(cb), _full(wd),
                  _full(fin_gain)],
        out_specs=row,
        out_shape=jax.ShapeDtypeStruct((tokens, D_MODEL), F32),
        scratch_shapes=[pltpu.VMEM((tm + HALO, D_MODEL), BF16), pltpu.VMEM((tm, D_MODEL), F32)],
        compiler_params=_cparams(1),
        name="conv_ffn",
    )(x, x, gain, wg, wu, cw, cb, wd, fin_gain)


def _rope_tables(seq):
    pos = jnp.arange(seq, dtype=F32)
    inv = ROPE_THETA ** (-jnp.arange(0, HEAD_DIM, 2, dtype=F32) / HEAD_DIM)
    ang = pos[:, None] * inv[None, :]
    cos, sin = jnp.cos(ang), jnp.sin(ang)
    reps = MXU_N // HEAD_DIM
    cos_t = jnp.tile(jnp.concatenate([cos, cos], axis=1), (1, reps))
    sin_t = jnp.tile(jnp.concatenate([-sin, sin], axis=1), (1, reps))
    return cos_t, sin_t


def _overlap_rows(seq):
    n_cmp = (seq - CMP_LEN) // CMP_STRIDE + 1
    n_slc = seq // SLC_LEN
    cstart = np.arange(LANES) * CMP_STRIDE
    sstart = np.arange(LANES) * SLC_LEN
    ovl = ((cstart[None, :] <= sstart[:, None] + SLC_LEN - 1)
           & (cstart[None, :] + CMP_LEN - 1 >= sstart[:, None])
           & (np.arange(LANES)[None, :] < n_cmp) & (np.arange(LANES)[:, None] < n_slc))
    ovl[n_slc, :] = True
    return jnp.asarray(ovl, dtype=BF16)


def _nsa_layer(x, gain, cos, sin, w_in, cmp_pe, cmp_w1, cmp_w2, gate_b, w_o, ovl, *, batch, seq):
    n_qkv = NSA_HEADS * HEAD_DIM + 6 * NSA_KV_HEADS * HEAD_DIM
    perm = np.array([(g * NSA_R + r) * 3 + j for g in range(NSA_KV_HEADS) for j in range(3)
                     for r in range(NSA_R)])
    pad = LANES - 3 * NSA_HEADS
    w = jnp.concatenate([w_in[:, :n_qkv], w_in[:, n_qkv:][:, perm],
                         jnp.zeros((D_MODEL, pad), F32)], axis=1).astype(BF16)
    gb = jnp.concatenate([gate_b[perm], jnp.zeros((pad,), F32)])[None, :]
    q, kc, vc, ks, vs, kw, vw, gates = _nsa_proj(x, gain[None, :], w, cos, sin, gb, batch=batch, seq=seq)

    rows16 = seq // CMP_STRIDE
    flat = lambda t: t.reshape(batch * NSA_KV_HEADS * rows16, CMP_STRIDE * HEAD_DIM)
    pe = cmp_pe.reshape(2, 2, CMP_STRIDE * HEAD_DIM)
    w2 = jnp.pad(cmp_w2, ((0, 0), (0, 0), (0, LANES - HEAD_DIM))).astype(BF16)
    kcc, vcc = _compress(flat(kc), flat(vc), pe, cmp_w1.astype(BF16), w2)
    kcc = kcc.reshape(batch, NSA_KV_HEADS, rows16, LANES)
    vcc = vcc.reshape(batch, NSA_KV_HEADS, rows16, LANES)

    o = _nsa_attn(q, kcc, vcc, ks, vs, kw, vw, gates, ovl, batch=batch, seq=seq)
    return _out_proj(o.reshape(batch * seq, NSA_HEADS * HEAD_DIM), x, w_o.astype(BF16))


def _swa_layer(x, gain, cos, sin, w_qkv, b_qkv, sinks, w_o, b_o, *, batch, seq):
    q, k, v = _swa_proj(x, gain[None, :], w_qkv.astype(BF16), b_qkv[None, :], cos, sin,
                        batch=batch, seq=seq)
    sink_rows = jnp.broadcast_to(
        jnp.repeat(sinks.reshape(SWA_KV_HEADS, SWA_R), SWA_TQ, axis=1)[:, :, None],
        (SWA_KV_HEADS, SWA_R * SWA_TQ, LANES))
    o = _swa_attn(q, k, v, sink_rows, batch=batch, seq=seq)
    return _out_proj(o.reshape(batch * seq, SWA_HEADS * HEAD_DIM), x, w_o.astype(BF16), b_o[None, :])


def _ffn_layer(x, gain, w_gu, conv_w, conv_b, w_down, fin_gain, *, seq, final_norm):
    chunks = lambda w: w.reshape(D_MODEL, N_FF_CHUNKS, FF_CHUNK).transpose(1, 0, 2).astype(BF16)
    wg = chunks(w_gu[:, :D_FF])
    wu = chunks(w_gu[:, D_FF:])
    wd = w_down.reshape(N_FF_CHUNKS, FF_CHUNK, D_MODEL).astype(BF16)
    cw = conv_w.reshape(CONV_WIDTH, N_FF_CHUNKS, FF_CHUNK).transpose(1, 0, 2)
    cb = conv_b.reshape(N_FF_CHUNKS, 1, FF_CHUNK)
    return _ffn(x, gain[None, :], wg, wu, cw, cb, wd, fin_gain[None, :], seq=seq, final_norm=final_norm)


def kernel(x, nsa_w_in, nsa_cmp_pe, nsa_cmp_w1, nsa_cmp_w2, nsa_gate_b, nsa_w_o,
           swa_w_qkv, swa_b_qkv, swa_sinks, swa_w_o, swa_b_o,
           ffn_w_gu, ffn_conv_w, ffn_conv_b, ffn_w_down,
           norm_mix, norm_ffn, norm_final):
    batch, seq, _ = x.shape
    depth = ffn_w_gu.shape[0]
    cos, sin = _rope_tables(seq)
    ovl = _overlap_rows(seq)
    h = x.reshape(batch * seq, D_MODEL)
    for i in range(depth):
        j = i // 2
        if i % 2 == 0:
            h = _nsa_layer(h, norm_mix[i], cos, sin, nsa_w_in[j], nsa_cmp_pe[j], nsa_cmp_w1[j],
                           nsa_cmp_w2[j], nsa_gate_b[j], nsa_w_o[j], ovl, batch=batch, seq=seq)
        else:
            h = _swa_layer(h, norm_mix[i], cos, sin, swa_w_qkv[j], swa_b_qkv[j], swa_sinks[j],
                           swa_w_o[j], swa_b_o[j], batch=batch, seq=seq)
        h = _ffn_layer(h, norm_ffn[i], ffn_w_gu[i], ffn_conv_w[i], ffn_conv_b[i], ffn_w_down[i],
                       norm_final, seq=seq, final_norm=(i == depth - 1))
    return h.reshape(batch, seq, D_MODEL)
```
